```python
import jax, jax.numpy as jnp
from jax import lax
import numpy as np

D_MODEL = 1024
BATCH = 8
SEQ = 8192
DEPTH = 1
DEC_BATCH = 16
DEC_SEQ = 4096
PAST_LEN = 128

GRID_W = 64
N_MEM = 256
F_GROUPS = 4
F_GROUP_DIM = 96
F_WIDTH = F_GROUPS * F_GROUP_DIM
NA_HEADS = 6
NA_HEAD_DIM = 64
NA_WIDTH = NA_HEADS * NA_HEAD_DIM
CA_HEADS = 4
CA_HEAD_DIM = 64
CA_WIDTH = CA_HEADS * CA_HEAD_DIM
NA_KH_MAX = 8
NA_KW = 16
NA_QBLK = 16
NA_KBLK = 32
N_BRANCH = 3
IN_SPLITS = (F_WIDTH, F_WIDTH, 3 * NA_WIDTH, NA_WIDTH, CA_WIDTH, CA_WIDTH, N_BRANCH * D_MODEL)
IN_WIDTH = F_WIDTH * 2 + NA_WIDTH * 4 + CA_WIDTH * 2 + N_BRANCH * D_MODEL
EPS = 1e-6
NEG = -1e30

kernel_name = "hybrid_fnet_natten_memory_encoder"


def rmsnorm(x, g):
    xf = x.astype(jnp.float32)
    y = xf * lax.rsqrt(jnp.mean(xf * xf, axis=-1, keepdims=True) + EPS)
    return (y * g.astype(jnp.float32)).astype(x.dtype)


def fourier_mix(u):
    B, S, _ = u.shape
    ug = u.reshape(B, S, F_GROUPS, F_GROUP_DIM).astype(jnp.float32)
    yr = jnp.fft.fft2(ug, axes=(1, 3), norm="ortho").real
    return yr.reshape(B, S, F_WIDTH).astype(u.dtype)


def na_indices(rows):
    kh = min(NA_KH_MAX, rows)
    r = np.arange(rows)
    rs = np.clip(r - kh // 2, 0, rows - kh)
    row_idx = rs[:, None] + np.arange(kh)[None, :]
    dr = row_idx - r[:, None] + (NA_KH_MAX - 1)
    nj = GRID_W // NA_QBLK
    j = np.arange(nj)
    c0 = np.clip(j * NA_QBLK - NA_KW // 2, 0, GRID_W - NA_KBLK)
    col_idx = c0[:, None] + np.arange(NA_KBLK)[None, :]
    c = j[:, None] * NA_QBLK + np.arange(NA_QBLK)[None, :]
    cs = np.clip(c - NA_KW // 2, 0, GRID_W - NA_KW)
    rel = col_idx[:, None, :] - cs[:, :, None]
    valid = (rel >= 0) & (rel < NA_KW)
    dc = np.clip(col_idx[:, None, :] - c[:, :, None] + (NA_KW - 1), 0, 2 * NA_KW - 2)
    return row_idx, dr, col_idx, valid, dc


def neighborhood_attention(q, k, v, rpb):
    B, S, H, Dh = q.shape
    rows = S // GRID_W
    row_idx, dr, col_idx, valid, dc = na_indices(rows)
    nj = GRID_W // NA_QBLK
    qb = q.reshape(B, rows, nj, NA_QBLK, H, Dh)
    ridx = row_idx[:, None, :, None]
    cidx = col_idx[None, :, None, :]
    kg = k.reshape(B, rows, GRID_W, H, Dh)[:, ridx, cidx]
    vg = v.reshape(B, rows, GRID_W, H, Dh)[:, ridx, cidx]
    s = jnp.einsum('brjqhd,brjkwhd->bhrjqkw', qb, kg).astype(jnp.float32) * (Dh ** -0.5)
    bias = rpb.astype(jnp.float32)[:, dr[:, None, None, :, None], dc[None, :, :, None, :]]
    mask = jnp.asarray(valid)[None, :, :, None, :]
    s = jnp.where(mask, s + bias[None], NEG)
    sh = s.shape
    p = jax.nn.softmax(s.reshape(sh[:5] + (sh[5] * sh[6],)), axis=-1).reshape(sh)
    o = jnp.einsum('bhrjqkw,brjkwhd->brjqhd', p.astype(v.dtype), vg)
    return o.reshape(B, S, H * Dh)


def memory_attention(q, k, v):
    B, S, H, Dh = q.shape
    s = jnp.einsum('bshd,bmhd->bhsm', q, k).astype(jnp.float32) * (Dh ** -0.5)
    p = jax.nn.softmax(s, axis=-1).astype(v.dtype)
    return jnp.einsum('bhsm,bmhd->bshd', p, v).reshape(B, S, H * Dh)


def hybrid_layer(x, mem, g_norm, w_in, rpb, g_mem, w_mem_kv, w_f_out, w_na_out, w_ca_out, w_out):
    B, S, D = x.shape
    h = rmsnorm(x, g_norm)
    z = h @ w_in
    cuts = [int(c) for c in np.cumsum(IN_SPLITS)[:-1]]
    u_f, gate_f, qkv_na, gate_na, q_ca, gate_ca, g_merge = jnp.split(z, cuts, axis=-1)
    y_f = (fourier_mix(u_f) * jax.nn.silu(gate_f)) @ w_f_out
    qkv = qkv_na.reshape(B, S, 3, NA_HEADS, NA_HEAD_DIM)
    o_na = neighborhood_attention(qkv[:, :, 0], qkv[:, :, 1], qkv[:, :, 2], rpb)
    y_na = (o_na * jax.nn.silu(gate_na)) @ w_na_out
    M = mem.shape[1]
    kv = (rmsnorm(mem, g_mem) @ w_mem_kv).reshape(B, M, 2, CA_HEADS, CA_HEAD_DIM)
    o_ca = memory_attention(q_ca.reshape(B, S, CA_HEADS, CA_HEAD_DIM), kv[:, :, 0], kv[:, :, 1])
    y_ca = (o_ca * jax.nn.silu(gate_ca)) @ w_ca_out
    gm = jax.nn.sigmoid(g_merge).reshape(B, S, N_BRANCH, D)
    merged = gm[:, :, 0] * y_f + gm[:, :, 1] * y_na + gm[:, :, 2] * y_ca
    return x + merged @ w_out


def trunk(x, mem, g_norm, w_in, na_rpb, g_mem, w_mem_kv, w_f_out, w_na_out, w_ca_out, w_out, g_final):
    for l in range(DEPTH):
        x = hybrid_layer(x, mem, g_norm[l], w_in[l], na_rpb[l], g_mem[l], w_mem_kv[l],
                         w_f_out[l], w_na_out[l], w_ca_out[l], w_out[l])
    return rmsnorm(x, g_final)


def setup_inputs(seed: int = 0) -> dict:
    key = jax.random.key(seed)
    ks = jax.random.split(key, 16)
    f32 = jnp.float32
    nrm = lambda k, shape, s: jax.random.normal(k, shape, f32) * s
    return {
        "x_prompt": nrm(ks[0], (BATCH, SEQ, D_MODEL), 1.0),
        "x_sample": nrm(ks[1], (DEC_BATCH, DEC_SEQ, D_MODEL), 1.0),
        "mem_prompt": nrm(ks[2], (BATCH, N_MEM, D_MODEL), 1.0),
        "mem_sample": nrm(ks[3], (DEC_BATCH, N_MEM, D_MODEL), 1.0),
        "g_norm": 1.0 + nrm(ks[4], (DEPTH, D_MODEL), 0.02),
        "w_in": nrm(ks[5], (DEPTH, D_MODEL, IN_WIDTH), D_MODEL ** -0.5),
        "na_rpb": nrm(ks[6], (DEPTH, NA_HEADS, 2 * NA_KH_MAX - 1, 2 * NA_KW - 1), 0.1),
        "g_mem": 1.0 + nrm(ks[7], (DEPTH, D_MODEL), 0.02),
        "w_mem_kv": nrm(ks[8], (DEPTH, D_MODEL, 2 * CA_WIDTH), D_MODEL ** -0.5),
        "w_f_out": nrm(ks[9], (DEPTH, F_WIDTH, D_MODEL), F_WIDTH ** -0.5),
        "w_na_out": nrm(ks[10], (DEPTH, NA_WIDTH, D_MODEL), NA_WIDTH ** -0.5),
        "w_ca_out": nrm(ks[11], (DEPTH, CA_WIDTH, D_MODEL), CA_WIDTH ** -0.5),
        "w_out": nrm(ks[12], (DEPTH, D_MODEL, D_MODEL), D_MODEL ** -0.5),
        "g_final": 1.0 + nrm(ks[13], (D_MODEL,), 0.02),
    }


def reference(x_prompt, x_sample, mem_prompt, mem_sample, g_norm, w_in, na_rpb, g_mem, w_mem_kv,
              w_f_out, w_na_out, w_ca_out, w_out, g_final):
    y_prompt = trunk(x_prompt, mem_prompt, g_norm, w_in, na_rpb, g_mem, w_mem_kv,
                     w_f_out, w_na_out, w_ca_out, w_out, g_final)
    y_sample = trunk(x_sample, mem_sample, g_norm, w_in, na_rpb, g_mem, w_mem_kv,
                     w_f_out, w_na_out, w_ca_out, w_out, g_final)
    return (y_prompt, y_sample)
```

```python
import functools

import numpy as np
import jax
import jax.numpy as jnp
from jax import lax
from jax.experimental import pallas as pl
from jax.experimental.pallas import tpu as pltpu

D_MODEL = 1024
GRID_W = 64
N_MEM = 256
F_GROUPS = 4
F_GROUP_DIM = 96
F_WIDTH = F_GROUPS * F_GROUP_DIM
NA_HEADS = 6
HEAD_DIM = 64
NA_WIDTH = NA_HEADS * HEAD_DIM
CA_HEADS = 4
CA_WIDTH = CA_HEADS * HEAD_DIM
NA_KH = 8
NA_KW = 16
EPS = 1e-6
NEG = -1e30

LANES = 128
BF16_SUBLANES = 16
NA_ROWS_PER_GROUP = 4
ROW_PITCH = 72
A_WIDTH = 2 * F_WIDTH + 4 * NA_WIDTH
C_WIDTH = 2 * CA_WIDTH + 3 * D_MODEL
VMEM_LIMIT = 56 * 1024 * 1024

BF16 = jnp.bfloat16
F32 = jnp.float32


def _rmsnorm(x, g):
    return x * lax.rsqrt(jnp.mean(x * x, axis=-1, keepdims=True) + EPS) * g


def _dot(a, b):
    return jnp.dot(a, b, preferred_element_type=F32)


def _dot_nt(a, b):
    return lax.dot_general(a, b, (((1,), (1,)), ((), ())), preferred_element_type=F32)


def _params(*semantics):
    return pltpu.CompilerParams(dimension_semantics=semantics, vmem_limit_bytes=VMEM_LIMIT)


def _pair_attention(q_pair, k_pair, v_pair, bias):
    m = q_pair.shape[0]
    first = lax.broadcasted_iota(jnp.int32, (m, LANES), 1) < HEAD_DIM
    zero = jnp.zeros_like(q_pair)
    q2 = jnp.concatenate([jnp.where(first, q_pair, zero), jnp.where(first, zero, q_pair)], axis=0)
    s = _dot_nt(q2, k_pair)
    if bias is not None:
        s = s + bias
    p = jnp.exp(s - jnp.max(s, axis=-1, keepdims=True))
    l = jnp.sum(p, axis=-1, keepdims=True)
    o2 = _dot(p.astype(BF16), v_pair) / l
    return jnp.where(first, o2[:m], o2[m:])


def _mem_kv_kernel(mem_ref, g_ref, w_ref, kv_ref):
    h = _rmsnorm(mem_ref[0], g_ref[...]).astype(BF16)
    kv_ref[0] = _dot(h, w_ref[...]).astype(BF16)


def _mem_kv(mem, g_mem, w_kv):
    b = mem.shape[0]
    return pl.pallas_call(
        _mem_kv_kernel,
        grid=(b,),
        in_specs=[pl.BlockSpec((1, N_MEM, D_MODEL), lambda i: (i, 0, 0)),
                  pl.BlockSpec((1, D_MODEL), lambda i: (0, 0)),
                  pl.BlockSpec((D_MODEL, 2 * CA_WIDTH), lambda i: (0, 0))],
        out_specs=pl.BlockSpec((1, N_MEM, 2 * CA_WIDTH), lambda i: (i, 0, 0)),
        out_shape=jax.ShapeDtypeStruct((b, N_MEM, 2 * CA_WIDTH), BF16),
        compiler_params=_params("arbitrary"),
        name="mem_kv",
    )(mem, g_mem, w_kv)


def _in_proj_kernel(x_ref, g_ref, w_ref, cdft_ref, p_ref, q_ref, gf_ref, qn_ref, kn_ref, vn_ref, gn_ref,
                    t_ref, *, grid_rows):
    h = _rmsnorm(x_ref[...], g_ref[...]).astype(BF16)
    w = F_WIDTH
    zcols = lambda j: _dot(h, w_ref[:, 2 * j * w:2 * (j + 1) * w])
    z_f = zcols(0)
    pq = _dot(z_f[:, :w].astype(BF16), cdft_ref[...])
    slabs = 2 * w // LANES
    for r in range(grid_rows):
        for s in range(slabs):
            t_ref[s, r * ROW_PITCH:r * ROW_PITCH + GRID_W, :] = pq[r * GRID_W:(r + 1) * GRID_W,
                                                                   s * LANES:(s + 1) * LANES]
    for c in range(GRID_W):
        for s in range(slabs):
            v = t_ref[s, pl.ds(c, grid_rows, stride=ROW_PITCH), :].astype(BF16)
            dst = p_ref if s < slabs // 2 else q_ref
            lane0 = c * w + (s % (slabs // 2)) * LANES
            dst[0, :, lane0:lane0 + LANES] = v
    gf_ref[...] = z_f[:, w:].astype(BF16)
    z_qk = zcols(1)
    qn_ref[...] = (z_qk[:, :w] * (HEAD_DIM ** -0.5)).astype(BF16)
    kn_ref[...] = z_qk[:, w:].astype(BF16)
    z_vg = zcols(2)
    vn_ref[...] = z_vg[:, :w].astype(BF16)
    gn_ref[...] = z_vg[:, w:].astype(BF16)


def _in_proj(x2, g_norm, w_a, cdft, seq, tm):
    n = x2.shape[0]
    grid_rows = tm // GRID_W
    per_batch = seq // tm
    out = jax.ShapeDtypeStruct((n, F_WIDTH), BF16)
    ospec = pl.BlockSpec((tm, F_WIDTH), lambda i: (i, 0))
    wide = jax.ShapeDtypeStruct((n // seq, seq // GRID_W, GRID_W * F_WIDTH), BF16)
    wspec = pl.BlockSpec((1, grid_rows, GRID_W * F_WIDTH), lambda i: (i // per_batch, i % per_batch, 0))
    return pl.pallas_call(
        functools.partial(_in_proj_kernel, grid_rows=grid_rows),
        grid=(n // tm,),
        in_specs=[pl.BlockSpec((tm, D_MODEL), lambda i: (i, 0)),
                  pl.BlockSpec((1, D_MODEL), lambda i: (0, 0)),
                  pl.BlockSpec((D_MODEL, A_WIDTH), lambda i: (0, 0)),
                  pl.BlockSpec((F_WIDTH, 2 * F_WIDTH), lambda i: (0, 0))],
        out_specs=[wspec, wspec] + [ospec] * 5,
        out_shape=[wide, wide] + [out] * 5,
        scratch_shapes=[pltpu.VMEM((2 * F_WIDTH // LANES, grid_rows * ROW_PITCH, LANES), F32)],
        compiler_params=_params("arbitrary"),
        name="in_proj",
    )(x2, g_norm, w_a, cdft)


def _fft_rows_kernel(p_ref, q_ref, w1_ref, w2_ref, tc_ref, ts_ref, ar_ref, ai_ref, *, rows, cols_per_step):
    for j in range(cols_per_step):
        sl = slice(j * F_WIDTH, (j + 1) * F_WIDTH)
        r = _dot(w1_ref[...], p_ref[0, :, sl]) + _dot(w2_ref[...], q_ref[0, :, sl])
        tc = tc_ref[:, j * LANES:(j + 1) * LANES]
        ts = ts_ref[:, j * LANES:(j + 1) * LANES]
        for cb in range(F_WIDTH // LANES):
            cs = slice(cb * LANES, (cb + 1) * LANES)
            a_r = r[:rows, cs]
            a_i = r[rows:, cs]
            ar_ref[0, j, :, cs] = (a_r * tc + a_i * ts).astype(BF16)
            ai_ref[0, j, :, cs] = (a_i * tc - a_r * ts).astype(BF16)


def _fft_rows(p3, q3, w1, w2, tc, ts, rows, cols_per_step):
    b = p3.shape[0]
    blk = cols_per_step * F_WIDTH
    out = jax.ShapeDtypeStruct((b, GRID_W, rows, F_WIDTH), BF16)
    ospec = pl.BlockSpec((1, cols_per_step, rows, F_WIDTH), lambda j, i: (i, j, 0, 0))
    return pl.pallas_call(
        functools.partial(_fft_rows_kernel, rows=rows, cols_per_step=cols_per_step),
        grid=(GRID_W // cols_per_step, b),
        in_specs=[pl.BlockSpec((1, rows, blk), lambda j, i: (i, 0, j)),
                  pl.BlockSpec((1, rows, blk), lambda j, i: (i, 0, j)),
                  pl.BlockSpec((2 * rows, rows), lambda j, i: (0, 0)),
                  pl.BlockSpec((2 * rows, rows), lambda j, i: (0, 0)),
                  pl.BlockSpec((rows, cols_per_step * LANES), lambda j, i: (0, j)),
                  pl.BlockSpec((rows, cols_per_step * LANES), lambda j, i: (0, j))],
        out_specs=[ospec, ospec],
        out_shape=[out, out],
        compiler_params=_params("arbitrary", "arbitrary"),
        name="fft_rows",
    )(p3, q3, w1, w2, tc, ts)


def _fft_cols_kernel(ar_ref, ai_ref, gate_ref, kc_ref, ks_ref, o_ref, *, norm, tiles):
    m = GRID_W * BF16_SUBLANES
    for t in range(tiles):
        sl = slice(t * BF16_SUBLANES, (t + 1) * BF16_SUBLANES)
        xr = ar_ref[0, :, sl, :].reshape(m, F_WIDTH)
        xi = ai_ref[0, :, sl, :].reshape(m, F_WIDTH)
        y = _dot(kc_ref[...], xr) + _dot(ks_ref[...], xi)
        g = gate_ref[0, :, sl, :].reshape(m, F_WIDTH).astype(F32)
        o_ref[0, :, sl, :] = (y * norm * jax.nn.silu(g)).astype(BF16).reshape(GRID_W, BF16_SUBLANES, F_WIDTH)


def _fft_cols(ar, ai, gate4, kc, ks, norm, tiles):
    b, _, rows, _ = ar.shape
    kb = tiles * BF16_SUBLANES
    m = GRID_W * BF16_SUBLANES
    spec = pl.BlockSpec((1, GRID_W, kb, F_WIDTH), lambda i, j: (i, 0, j, 0))
    wspec = pl.BlockSpec((m, m), lambda i, j: (0, 0))
    return pl.pallas_call(
        functools.partial(_fft_cols_kernel, norm=norm, tiles=tiles),
        grid=(b, rows // kb),
        in_specs=[spec, spec, spec, wspec, wspec],
        out_specs=spec,
        out_shape=jax.ShapeDtypeStruct((b, GRID_W, rows, F_WIDTH), BF16),
        compiler_params=_params("arbitrary", "arbitrary"),
        name="fft_cols",
    )(ar, ai, gate4, kc, ks)


def _na_kernel(q_ref, k_ref, v_ref, gate_ref, bias_ref, o_ref, *, rows, rows_per_step):
    i = pl.program_id(1)
    first = lax.broadcasted_iota(jnp.int32, (GRID_W, LANES), 1) < HEAD_DIM

    def group_body(t2, carry):
        chains = []
        for dt in range(NA_ROWS_PER_GROUP):
            t = t2 * NA_ROWS_PER_GROUP + dt
            r = i * rows_per_step + t
            rs = jnp.clip(r - NA_KH // 2, 0, rows - NA_KH)
            tok = pl.ds(pl.multiple_of(t * GRID_W, GRID_W), GRID_W)
            keys = pl.ds(pl.multiple_of(rs * GRID_W, GRID_W), NA_KH * GRID_W)
            for hp in range(NA_HEADS // 2):
                chains.append((tok, keys, slice(hp * LANES, (hp + 1) * LANES), hp, r - rs))
        scores = []
        for tok, keys, cs, hp, delta in chains:
            q_pair = q_ref[0, tok, cs]
            zero = jnp.zeros_like(q_pair)
            q2 = jnp.concatenate([jnp.where(first, q_pair, zero), jnp.where(first, zero, q_pair)], axis=0)
            scores.append(_dot_nt(q2, k_ref[0, keys, cs]) + bias_ref[hp, delta])
        probs = [jnp.exp(s - jnp.max(s, axis=-1, keepdims=True)) for s in scores]
        sums = [jnp.sum(p, axis=-1, keepdims=True) for p in probs]
        outs = [_dot(p.astype(BF16), v_ref[0, keys, cs]) / l
                for p, l, (tok, keys, cs, hp, delta) in zip(probs, sums, chains)]
        for o2, (tok, keys, cs, hp, delta) in zip(outs, chains):
            o = jnp.where(first, o2[:GRID_W], o2[GRID_W:])
            g = gate_ref[0, tok, cs].astype(F32)
            o_ref[0, tok, cs] = (o * jax.nn.silu(g)).astype(BF16)
        return carry

    lax.fori_loop(0, rows_per_step // NA_ROWS_PER_GROUP, group_body, 0)


def _na(q3, k3, v3, gate3, bias, rows_per_step):
    b, s, _ = q3.shape
    rows = s // GRID_W
    tm = rows_per_step * GRID_W
    tile = pl.BlockSpec((1, tm, NA_WIDTH), lambda i, j: (i, j, 0))
    full = pl.BlockSpec((1, s, NA_WIDTH), lambda i, j: (i, 0, 0))
    return pl.pallas_call(
        functools.partial(_na_kernel, rows=rows, rows_per_step=rows_per_step),
        grid=(b, rows // rows_per_step),
        in_specs=[tile, full, full, tile,
                  pl.BlockSpec(bias.shape, lambda i, j: (0, 0, 0, 0))],
        out_specs=tile,
        out_shape=jax.ShapeDtypeStruct((b, s, NA_WIDTH), BF16),
        compiler_params=_params("arbitrary", "arbitrary"),
        name="na_attn",
    )(q3, k3, v3, gate3, bias)


def _na_bias(rpb):
    c = np.arange(GRID_W)
    cs = np.clip(c - NA_KW // 2, 0, GRID_W - NA_KW)
    kc = np.arange(GRID_W)
    valid = (kc[None, :] >= cs[:, None]) & (kc[None, :] < cs[:, None] + NA_KW)
    dc = np.clip(kc[None, :] - c[:, None] + (NA_KW - 1), 0, 2 * NA_KW - 2)
    n_dr, n_dc = 2 * NA_KH - 1, 2 * NA_KW - 1
    onehot = (dc.reshape(-1)[None, :] == np.arange(n_dc)[:, None]).astype(np.float32)
    sel = jnp.dot(rpb.astype(F32).reshape(NA_HEADS * n_dr, n_dc), jnp.asarray(onehot),
                  precision=lax.Precision.HIGHEST)
    sel = sel.reshape(NA_HEADS, n_dr, GRID_W, GRID_W)
    sel = jnp.where(jnp.asarray(valid)[None, None], sel, NEG)
    per_delta = jnp.stack([sel[:, NA_KH - 1 - d:2 * NA_KH - 1 - d] for d in range(NA_KH)], axis=1)
    b = per_delta.transpose(0, 1, 3, 2, 4)
    b = b.reshape(NA_HEADS // 2, 2, NA_KH, GRID_W, NA_KH * GRID_W)
    return b.transpose(0, 2, 1, 3, 4).reshape(NA_HEADS // 2, NA_KH, 2 * GRID_W, NA_KH * GRID_W)


def _merge_kernel(x_ref, af_ref, ana_ref, km_ref, vm_ref, g_ref, wc_ref, wf_ref, wna_ref, wca_ref,
                  wout_ref, gfin_ref, y_ref):
    x = x_ref[...]
    h = _rmsnorm(x, g_ref[...]).astype(BF16)
    z = _dot(h, wc_ref[...])
    q_ca = (z[:, :CA_WIDTH] * (HEAD_DIM ** -0.5)).astype(BF16)
    parts = []
    for hp in range(CA_HEADS // 2):
        cs = slice(hp * LANES, (hp + 1) * LANES)
        o = _pair_attention(q_ca[:, cs], km_ref[0, :, cs], vm_ref[0, :, cs], None)
        parts.append((o * jax.nn.silu(z[:, CA_WIDTH + hp * LANES:CA_WIDTH + (hp + 1) * LANES])).astype(BF16))
    y_ca = _dot(jnp.concatenate(parts, axis=1), wca_ref[...])
    y_f = _dot(af_ref[...], wf_ref[...])
    y_na = _dot(ana_ref[...], wna_ref[...])
    m0 = 2 * CA_WIDTH
    merged = (jax.nn.sigmoid(z[:, m0:m0 + D_MODEL]) * y_f
              + jax.nn.sigmoid(z[:, m0 + D_MODEL:m0 + 2 * D_MODEL]) * y_na
              + jax.nn.sigmoid(z[:, m0 + 2 * D_MODEL:]) * y_ca)
    out = x + _dot(merged.astype(BF16), wout_ref[...])
    y_ref[...] = _rmsnorm(out, gfin_ref[...])


def _merge(x2, a_f, a_na, kv, g_norm, w_c, w_f, w_na, w_ca, w_out, g_final, seq, tm):
    n = x2.shape[0]
    per_batch = seq // tm
    const = lambda i: (0, 0)
    return pl.pallas_call(
        _merge_kernel,
        grid=(n // tm,),
        in_specs=[pl.BlockSpec((tm, D_MODEL), lambda i: (i, 0)),
                  pl.BlockSpec((tm, F_WIDTH), lambda i: (i, 0)),
                  pl.BlockSpec((tm, NA_WIDTH), lambda i: (i, 0)),
                  pl.BlockSpec((1, N_MEM, CA_WIDTH), lambda i: (i // per_batch, 0, 0)),
                  pl.BlockSpec((1, N_MEM, CA_WIDTH), lambda i: (i // per_batch, 0, 1)),
                  pl.BlockSpec((1, D_MODEL), const),
                  pl.BlockSpec((D_MODEL, C_WIDTH), const),
                  pl.BlockSpec((F_WIDTH, D_MODEL), const),
                  pl.BlockSpec((NA_WIDTH, D_MODEL), const),
                  pl.BlockSpec((CA_WIDTH, D_MODEL), const),
                  pl.BlockSpec((D_MODEL, D_MODEL), const),
                  pl.BlockSpec((1, D_MODEL), const)],
        out_specs=pl.BlockSpec((tm, D_MODEL), lambda i: (i, 0)),
        out_shape=jax.ShapeDtypeStruct((n, D_MODEL), F32),
        compiler_params=_params("arbitrary"),
        name="merge_out",
    )(x2, a_f, a_na, kv, kv, g_norm, w_c, w_f, w_na, w_ca, w_out, g_final)


def _cos_sin(n):
    k = np.arange(n)
    ang = 2.0 * np.pi * ((k[:, None] * k[None, :]) % n) / n
    return np.cos(ang).astype(np.float32), np.sin(ang).astype(np.float32)


def _row_tables(rows):
    seq = rows * GRID_W
    c, s = _cos_sin(rows)
    w1 = jnp.asarray(np.concatenate([c, -s], axis=0)).astype(BF16)
    w2 = jnp.asarray(np.concatenate([-s, -c], axis=0)).astype(BF16)
    idx = (lax.broadcasted_iota(jnp.int32, (rows, GRID_W), 0)
           * lax.broadcasted_iota(jnp.int32, (rows, GRID_W), 1)) % seq
    tw = idx.astype(F32) * (2.0 * np.pi / seq)
    tc = jnp.repeat(jnp.cos(tw), LANES, axis=1)
    ts = jnp.repeat(jnp.sin(tw), LANES, axis=1)
    return w1, w2, tc, ts


def _shared_tables():
    c, s = _cos_sin(F_GROUP_DIM)
    eye = np.eye(F_GROUPS, dtype=np.float32)
    cdft = jnp.asarray(np.concatenate([np.kron(eye, c), np.kron(eye, s)], axis=1)).astype(BF16)
    c, s = _cos_sin(GRID_W)
    eye = jnp.eye(BF16_SUBLANES, dtype=F32)
    m = GRID_W * BF16_SUBLANES
    kron = lambda a: (jnp.asarray(a)[:, None, :, None] * eye[None, :, None, :]).reshape(m, m).astype(BF16)
    return cdft, kron(c), kron(s)


def _trunk(x, mem, g_norm, w_a, w_c, bias, g_mem, w_kv, w_f, w_na, w_ca, w_out, g_final, tables):
    cdft, kc, ks = tables
    b, seq, d = x.shape
    rows = seq // GRID_W
    n = b * seq
    x2 = x.reshape(n, d)
    w1, w2, tc, ts = _row_tables(rows)

    kv = _mem_kv(mem, g_mem, w_kv)
    p, q, gate_f, qn, kn, vn, gate_na = _in_proj(x2, g_norm, w_a, cdft, seq, tm=1024)

    ar, ai = _fft_rows(p, q, w1, w2, tc, ts, rows, cols_per_step=8)
    a_f = _fft_cols(ar, ai, gate_f.reshape(b, GRID_W, rows, F_WIDTH), kc, ks,
                    norm=float((seq * F_GROUP_DIM) ** -0.5), tiles=2 if rows % (2 * BF16_SUBLANES) == 0 else 1)

    shp = (b, seq, NA_WIDTH)
    a_na = _na(qn.reshape(shp), kn.reshape(shp), vn.reshape(shp), gate_na.reshape(shp), bias, rows_per_step=8)

    y = _merge(x2, a_f.reshape(n, F_WIDTH), a_na.reshape(n, NA_WIDTH), kv, g_norm, w_c, w_f, w_na, w_ca,
               w_out, g_final, seq, tm=256)
    return y.reshape(b, seq, d)


def kernel(x_prompt, x_sample, mem_prompt, mem_sample, g_norm, w_in, na_rpb, g_mem, w_mem_kv,
           w_f_out, w_na_out, w_ca_out, w_out, g_final):
    assert w_in.shape[0] == 1, "the fused final norm assumes a single layer"
    w_a = w_in[0, :, :A_WIDTH].astype(BF16)
    w_c = w_in[0, :, A_WIDTH:].astype(BF16)
    args = (g_norm[0][None], w_a, w_c, _na_bias(na_rpb[0]), g_mem[0][None], w_mem_kv[0].astype(BF16),
            w_f_out[0].astype(BF16), w_na_out[0].astype(BF16), w_ca_out[0].astype(BF16),
            w_out[0].astype(BF16), g_final[None], _shared_tables())
    return (_trunk(x_prompt, mem_prompt, *args), _trunk(x_sample, mem_sample, *args))
```

```python
import functools

import numpy as np
import jax
import jax.numpy as jnp
from jax import lax
from jax.experimental import pallas as pl
from jax.experimental.pallas import tpu as pltpu

D_MODEL = 1024
GRID_W = 64
N_MEM = 256
F_GROUPS = 4
F_GROUP_DIM = 96
F_WIDTH = F_GROUPS * F_GROUP_DIM
NA_HEADS = 6
HEAD_DIM = 64
NA_WIDTH = NA_HEADS * HEAD_DIM
CA_HEADS = 4
CA_WIDTH = CA_HEADS * HEAD_DIM
NA_KH = 8
NA_KW = 16
EPS = 1e-6
NEG = -1e30

LANES = 128
BF16_SUBLANES = 16
NA_ROWS_PER_GROUP = 4
ROW_PITCH = 72
A_WIDTH = 2 * F_WIDTH + 4 * NA_WIDTH
C_WIDTH = 2 * CA_WIDTH + 3 * D_MODEL
VMEM_LIMIT = 56 * 1024 * 1024

BF16 = jnp.bfloat16
F32 = jnp.float32


def _rmsnorm(x, g):
    return x * lax.rsqrt(jnp.mean(x * x, axis=-1, keepdims=True) + EPS) * g


def _dot(a, b):
    return jnp.dot(a, b, preferred_element_type=F32)


def _dot_nt(a, b):
    return lax.dot_general(a, b, (((1,), (1,)), ((), ())), preferred_element_type=F32)


def _params(*semantics):
    return pltpu.CompilerParams(dimension_semantics=semantics, vmem_limit_bytes=VMEM_LIMIT)


def _pair_attention(q_pair, k_pair, v_pair, bias):
    m = q_pair.shape[0]
    first = lax.broadcasted_iota(jnp.int32, (m, LANES), 1) < HEAD_DIM
    zero = jnp.zeros_like(q_pair)
    q2 = jnp.concatenate([jnp.where(first, q_pair, zero), jnp.where(first, zero, q_pair)], axis=0)
    s = _dot_nt(q2, k_pair)
    if bias is not None:
        s = s + bias
    p = jnp.exp(s - jnp.max(s, axis=-1, keepdims=True))
    l = jnp.sum(p, axis=-1, keepdims=True)
    o2 = _dot(p.astype(BF16), v_pair) / l
    return jnp.where(first, o2[:m], o2[m:])


def _mem_kv_kernel(mem_ref, g_ref, w_ref, kv_ref):
    h = _rmsnorm(mem_ref[0], g_ref[...]).astype(BF16)
    kv_ref[0] = _dot(h, w_ref[...]).astype(BF16)


def _mem_kv(mem, g_mem, w_kv):
    b = mem.shape[0]
    return pl.pallas_call(
        _mem_kv_kernel,
        grid=(b,),
        in_specs=[pl.BlockSpec((1, N_MEM, D_MODEL), lambda i: (i, 0, 0)),
                  pl.BlockSpec((1, D_MODEL), lambda i: (0, 0)),
                  pl.BlockSpec((D_MODEL, 2 * CA_WIDTH), lambda i: (0, 0))],
        out_specs=pl.BlockSpec((1, N_MEM, 2 * CA_WIDTH), lambda i: (i, 0, 0)),
        out_shape=jax.ShapeDtypeStruct((b, N_MEM, 2 * CA_WIDTH), BF16),
        compiler_params=_params("arbitrary"),
        name="mem_kv",
    )(mem, g_mem, w_kv)


def _in_proj_kernel(x_ref, g_ref, w_ref, cdft_ref, p_ref, q_ref, gf_ref, qn_ref, kn_ref, vn_ref, gn_ref,
                    t_ref, *, grid_rows):
    h = _rmsnorm(x_ref[...], g_ref[...]).astype(BF16)
    w = F_WIDTH
    zcols = lambda j: _dot(h, w_ref[:, 2 * j * w:2 * (j + 1) * w])
    z_f = zcols(0)
    pq = _dot(z_f[:, :w].astype(BF16), cdft_ref[...])
    slabs = 2 * w // LANES
    for r in range(grid_rows):
        for s in range(slabs):
            t_ref[s, r * ROW_PITCH:r * ROW_PITCH + GRID_W, :] = pq[r * GRID_W:(r + 1) * GRID_W,
                                                                   s * LANES:(s + 1) * LANES]
    for c in range(GRID_W):
        for s in range(slabs):
            v = t_ref[s, pl.ds(c, grid_rows, stride=ROW_PITCH), :].astype(BF16)
            dst = p_ref if s < slabs // 2 else q_ref
            lane0 = c * w + (s % (slabs // 2)) * LANES
            dst[0, :, lane0:lane0 + LANES] = v
    gf_ref[...] = z_f[:, w:].astype(BF16)
    z_qk = zcols(1)
    qn_ref[...] = (z_qk[:, :w] * (HEAD_DIM ** -0.5)).astype(BF16)
    kn_ref[...] = z_qk[:, w:].astype(BF16)
    z_vg = zcols(2)
    vn_ref[...] = z_vg[:, :w].astype(BF16)
    gn_ref[...] = z_vg[:, w:].astype(BF16)


def _in_proj(x2, g_norm, w_a, cdft, seq, tm):
    n = x2.shape[0]
    grid_rows = tm // GRID_W
    per_batch = seq // tm
    out = jax.ShapeDtypeStruct((n, F_WIDTH), BF16)
    ospec = pl.BlockSpec((tm, F_WIDTH), lambda i: (i, 0))
    wide = jax.ShapeDtypeStruct((n // seq, seq // GRID_W, GRID_W * F_WIDTH), BF16)
    wspec = pl.BlockSpec((1, grid_rows, GRID_W * F_WIDTH), lambda i: (i // per_batch, i % per_batch, 0))
    return pl.pallas_call(
        functools.partial(_in_proj_kernel, grid_rows=grid_rows),
        grid=(n // tm,),
        in_specs=[pl.BlockSpec((tm, D_MODEL), lambda i: (i, 0)),
                  pl.BlockSpec((1, D_MODEL), lambda i: (0, 0)),
                  pl.BlockSpec((D_MODEL, A_WIDTH), lambda i: (0, 0)),
                  pl.BlockSpec((F_WIDTH, 2 * F_WIDTH), lambda i: (0, 0))],
        out_specs=[wspec, wspec] + [ospec] * 5,
        out_shape=[wide, wide] + [out] * 5,
        scratch_shapes=[pltpu.VMEM((2 * F_WIDTH // LANES, grid_rows * ROW_PITCH, LANES), F32)],
        compiler_params=_params("arbitrary"),
        name="in_proj",
    )(x2, g_norm, w_a, cdft)


def _fft_rows_kernel(p_ref, q_ref, w1_ref, w2_ref, tc_ref, ts_ref, ar_ref, ai_ref, *, rows, cols_per_step):
    for j in range(cols_per_step):
        sl = slice(j * F_WIDTH, (j + 1) * F_WIDTH)
        r = _dot(w1_ref[...], p_ref[0, :, sl]) + _dot(w2_ref[...], q_ref[0, :, sl])
        tc = tc_ref[:, j * LANES:(j + 1) * LANES]
        ts = ts_ref[:, j * LANES:(j + 1) * LANES]
        for cb in range(F_WIDTH // LANES):
            cs = slice(cb * LANES, (cb + 1) * LANES)
            a_r = r[:rows, cs]
            a_i = r[rows:, cs]
            ar_ref[0, j, :, cs] = (a_r * tc + a_i * ts).astype(BF16)
            ai_ref[0, j, :, cs] = (a_i * tc - a_r * ts).astype(BF16)


def _fft_rows(p3, q3, w1, w2, tc, ts, rows, cols_per_step):
    b = p3.shape[0]
    blk = cols_per_step * F_WIDTH
    out = jax.ShapeDtypeStruct((b, GRID_W, rows, F_WIDTH), BF16)
    ospec = pl.BlockSpec((1, cols_per_step, rows, F_WIDTH), lambda j, i: (i, j, 0, 0))
    return pl.pallas_call(
        functools.partial(_fft_rows_kernel, rows=rows, cols_per_step=cols_per_step),
        grid=(GRID_W // cols_per_step, b),
        in_specs=[pl.BlockSpec((1, rows, blk), lambda j, i: (i, 0, j)),
                  pl.BlockSpec((1, rows, blk), lambda j, i: (i, 0, j)),
                  pl.BlockSpec((2 * rows, rows), lambda j, i: (0, 0)),
                  pl.BlockSpec((2 * rows, rows), lambda j, i: (0, 0)),
                  pl.BlockSpec((rows, cols_per_step * LANES), lambda j, i: (0, j)),
                  pl.BlockSpec((rows, cols_per_step * LANES), lambda j, i: (0, j))],
        out_specs=[ospec, ospec],
        out_shape=[out, out],
        compiler_params=_params("arbitrary", "arbitrary"),
        name="fft_rows",
    )(p3, q3, w1, w2, tc, ts)


def _fft_cols_kernel(ar_ref, ai_ref, gate_ref, kc_ref, ks_ref, o_ref, *, norm, tiles):
    m = GRID_W * BF16_SUBLANES
    tile = lambda ref, t: ref[0, :, t * BF16_SUBLANES:(t + 1) * BF16_SUBLANES, :].reshape(m, F_WIDTH)
    xr = jnp.concatenate([tile(ar_ref, t) for t in range(tiles)], axis=1)
    xi = jnp.concatenate([tile(ai_ref, t) for t in range(tiles)], axis=1)
    y = _dot(kc_ref[...], xr) + _dot(ks_ref[...], xi)
    for t in range(tiles):
        g = tile(gate_ref, t).astype(F32)
        out = (y[:, t * F_WIDTH:(t + 1) * F_WIDTH] * norm * jax.nn.silu(g)).astype(BF16)
        o_ref[0, :, t * BF16_SUBLANES:(t + 1) * BF16_SUBLANES, :] = out.reshape(GRID_W, BF16_SUBLANES, F_WIDTH)


def _fft_cols(ar, ai, gate4, kc, ks, norm, tiles):
    b, _, rows, _ = ar.shape
    kb = tiles * BF16_SUBLANES
    m = GRID_W * BF16_SUBLANES
    spec = pl.BlockSpec((1, GRID_W, kb, F_WIDTH), lambda i, j: (i, 0, j, 0))
    wspec = pl.BlockSpec((m, m), lambda i, j: (0, 0))
    return pl.pallas_call(
        functools.partial(_fft_cols_kernel, norm=norm, tiles=tiles),
        grid=(b, rows // kb),
        in_specs=[spec, spec, spec, wspec, wspec],
        out_specs=spec,
        out_shape=jax.ShapeDtypeStruct((b, GRID_W, rows, F_WIDTH), BF16),
        compiler_params=_params("arbitrary", "arbitrary"),
        name="fft_cols",
    )(ar, ai, gate4, kc, ks)


def _na_kernel(q_ref, k_ref, v_ref, gate_ref, bias_ref, o_ref, *, rows, rows_per_step):
    i = pl.program_id(1)
    first = lax.broadcasted_iota(jnp.int32, (GRID_W, LANES), 1) < HEAD_DIM

    def group_body(t2, carry):
        chains = []
        for dt in range(NA_ROWS_PER_GROUP):
            t = t2 * NA_ROWS_PER_GROUP + dt
            r = i * rows_per_step + t
            rs = jnp.clip(r - NA_KH // 2, 0, rows - NA_KH)
            tok = pl.ds(pl.multiple_of(t * GRID_W, GRID_W), GRID_W)
            keys = pl.ds(pl.multiple_of(rs * GRID_W, GRID_W), NA_KH * GRID_W)
            for hp in range(NA_HEADS // 2):
                chains.append((tok, keys, slice(hp * LANES, (hp + 1) * LANES), hp, r - rs))
        scores = []
        for tok, keys, cs, hp, delta in chains:
            q_pair = q_ref[0, tok, cs]
            zero = jnp.zeros_like(q_pair)
            q2 = jnp.concatenate([jnp.where(first, q_pair, zero), jnp.where(first, zero, q_pair)], axis=0)
            scores.append(_dot_nt(q2, k_ref[0, keys, cs]) + bias_ref[hp, delta])
        probs = [jnp.exp(s - jnp.max(s, axis=-1, keepdims=True)) for s in scores]
        sums = [jnp.sum(p, axis=-1, keepdims=True) for p in probs]
        outs = [_dot(p.astype(BF16), v_ref[0, keys, cs]) / l
                for p, l, (tok, keys, cs, hp, delta) in zip(probs, sums, chains)]
        for o2, (tok, keys, cs, hp, delta) in zip(outs, chains):
            o = jnp.where(first, o2[:GRID_W], o2[GRID_W:])
            g = gate_ref[0, tok, cs].astype(F32)
            o_ref[0, tok, cs] = (o * jax.nn.silu(g)).astype(BF16)
        return carry

    lax.fori_loop(0, rows_per_step // NA_ROWS_PER_GROUP, group_body, 0)


def _na(q3, k3, v3, gate3, bias, rows_per_step):
    b, s, _ = q3.shape
    rows = s // GRID_W
    tm = rows_per_step * GRID_W
    tile = pl.BlockSpec((1, tm, NA_WIDTH), lambda i, j: (i, j, 0))
    full = pl.BlockSpec((1, s, NA_WIDTH), lambda i, j: (i, 0, 0))
    return pl.pallas_call(
        functools.partial(_na_kernel, rows=rows, rows_per_step=rows_per_step),
        grid=(b, rows // rows_per_step),
        in_specs=[tile, full, full, tile,
                  pl.BlockSpec(bias.shape, lambda i, j: (0, 0, 0, 0))],
        out_specs=tile,
        out_shape=jax.ShapeDtypeStruct((b, s, NA_WIDTH), BF16),
        compiler_params=_params("arbitrary", "arbitrary"),
        name="na_attn",
    )(q3, k3, v3, gate3, bias)


def _na_bias(rpb):
    c = np.arange(GRID_W)
    cs = np.clip(c - NA_KW // 2, 0, GRID_W - NA_KW)
    kc = np.arange(GRID_W)
    valid = (kc[None, :] >= cs[:, None]) & (kc[None, :] < cs[:, None] + NA_KW)
    dc = np.clip(kc[None, :] - c[:, None] + (NA_KW - 1), 0, 2 * NA_KW - 2)
    n_dr, n_dc = 2 * NA_KH - 1, 2 * NA_KW - 1
    onehot = (dc.reshape(-1)[None, :] == np.arange(n_dc)[:, None]).astype(np.float32)
    sel = jnp.dot(rpb.astype(F32).reshape(NA_HEADS * n_dr, n_dc), jnp.asarray(onehot),
                  precision=lax.Precision.HIGHEST)
    sel = sel.reshape(NA_HEADS, n_dr, GRID_W, GRID_W)
    sel = jnp.where(jnp.asarray(valid)[None, None], sel, NEG)
    per_delta = jnp.stack([sel[:, NA_KH - 1 - d:2 * NA_KH - 1 - d] for d in range(NA_KH)], axis=1)
    b = per_delta.transpose(0, 1, 3, 2, 4)
    b = b.reshape(NA_HEADS // 2, 2, NA_KH, GRID_W, NA_KH * GRID_W)
    return b.transpose(0, 2, 1, 3, 4).reshape(NA_HEADS // 2, NA_KH, 2 * GRID_W, NA_KH * GRID_W)


def _merge_kernel(x_ref, af_ref, ana_ref, km_ref, vm_ref, g_ref, wc_ref, wf_ref, wna_ref, wca_ref,
                  wout_ref, gfin_ref, y_ref):
    x = x_ref[...]
    h = _rmsnorm(x, g_ref[...]).astype(BF16)
    m0 = 2 * CA_WIDTH
    gate = lambda j: jax.nn.sigmoid(_dot(h, wc_ref[:, m0 + j * D_MODEL:m0 + (j + 1) * D_MODEL]))
    merged = gate(0) * _dot(af_ref[...], wf_ref[...])
    merged = merged + gate(1) * _dot(ana_ref[...], wna_ref[...])
    z_ca = _dot(h, wc_ref[:, :m0])
    q_ca = (z_ca[:, :CA_WIDTH] * (HEAD_DIM ** -0.5)).astype(BF16)
    parts = []
    for hp in range(CA_HEADS // 2):
        cs = slice(hp * LANES, (hp + 1) * LANES)
        o = _pair_attention(q_ca[:, cs], km_ref[0, :, cs], vm_ref[0, :, cs], None)
        parts.append((o * jax.nn.silu(z_ca[:, CA_WIDTH + hp * LANES:CA_WIDTH + (hp + 1) * LANES])).astype(BF16))
    merged = merged + gate(2) * _dot(jnp.concatenate(parts, axis=1), wca_ref[...])
    out = x + _dot(merged.astype(BF16), wout_ref[...])
    y_ref[...] = _rmsnorm(out, gfin_ref[...])


def _merge(x2, a_f, a_na, kv, g_norm, w_c, w_f, w_na, w_ca, w_out, g_final, seq, tm):
    n = x2.shape[0]
    per_batch = seq // tm
    const = lambda i: (0, 0)
    return pl.pallas_call(
        _merge_kernel,
        grid=(n // tm,),
        in_specs=[pl.BlockSpec((tm, D_MODEL), lambda i: (i, 0)),
                  pl.BlockSpec((tm, F_WIDTH), lambda i: (i, 0)),
                  pl.BlockSpec((tm, NA_WIDTH), lambda i: (i, 0)),
                  pl.BlockSpec((1, N_MEM, CA_WIDTH), lambda i: (i // per_batch, 0, 0)),
                  pl.BlockSpec((1, N_MEM, CA_WIDTH), lambda i: (i // per_batch, 0, 1)),
                  pl.BlockSpec((1, D_MODEL), const),
                  pl.BlockSpec((D_MODEL, C_WIDTH), const),
                  pl.BlockSpec((F_WIDTH, D_MODEL), const),
                  pl.BlockSpec((NA_WIDTH, D_MODEL), const),
                  pl.BlockSpec((CA_WIDTH, D_MODEL), const),
                  pl.BlockSpec((D_MODEL, D_MODEL), const),
                  pl.BlockSpec((1, D_MODEL), const)],
        out_specs=pl.BlockSpec((tm, D_MODEL), lambda i: (i, 0)),
        out_shape=jax.ShapeDtypeStruct((n, D_MODEL), F32),
        compiler_params=_params("arbitrary"),
        name="merge_out",
    )(x2, a_f, a_na, kv, kv, g_norm, w_c, w_f, w_na, w_ca, w_out, g_final)


def _cos_sin(n):
    k = np.arange(n)
    ang = 2.0 * np.pi * ((k[:, None] * k[None, :]) % n) / n
    return np.cos(ang).astype(np.float32), np.sin(ang).astype(np.float32)


def _row_tables(rows):
    seq = rows * GRID_W
    c, s = _cos_sin(rows)
    w1 = jnp.asarray(np.concatenate([c, -s], axis=0)).astype(BF16)
    w2 = jnp.asarray(np.concatenate([-s, -c], axis=0)).astype(BF16)
    idx = (lax.broadcasted_iota(jnp.int32, (rows, GRID_W), 0)
           * lax.broadcasted_iota(jnp.int32, (rows, GRID_W), 1)) % seq
    tw = idx.astype(F32) * (2.0 * np.pi / seq)
    tc = jnp.repeat(jnp.cos(tw), LANES, axis=1)
    ts = jnp.repeat(jnp.sin(tw), LANES, axis=1)
    return w1, w2, tc, ts


def _shared_tables():
    c, s = _cos_sin(F_GROUP_DIM)
    eye = np.eye(F_GROUPS, dtype=np.float32)
    cdft = jnp.asarray(np.concatenate([np.kron(eye, c), np.kron(eye, s)], axis=1)).astype(BF16)
    c, s = _cos_sin(GRID_W)
    eye = np.eye(BF16_SUBLANES, dtype=np.float32)
    return cdft, jnp.asarray(np.kron(c, eye)).astype(BF16), jnp.asarray(np.kron(s, eye)).astype(BF16)


def _trunk(x, mem, g_norm, w_a, w_c, bias, g_mem, w_kv, w_f, w_na, w_ca, w_out, g_final, tables):
    cdft, kc, ks = tables
    b, seq, d = x.shape
    rows = seq // GRID_W
    n = b * seq
    x2 = x.reshape(n, d)
    w1, w2, tc, ts = _row_tables(rows)

    kv = _mem_kv(mem, g_mem, w_kv)
    p, q, gate_f, qn, kn, vn, gate_na = _in_proj(x2, g_norm, w_a, cdft, seq, tm=1024)

    ar, ai = _fft_rows(p, q, w1, w2, tc, ts, rows, cols_per_step=16)
    a_f = _fft_cols(ar, ai, gate_f.reshape(b, GRID_W, rows, F_WIDTH), kc, ks,
                    norm=float((seq * F_GROUP_DIM) ** -0.5), tiles=2 if rows % (2 * BF16_SUBLANES) == 0 else 1)

    shp = (b, seq, NA_WIDTH)
    a_na = _na(qn.reshape(shp), kn.reshape(shp), vn.reshape(shp), gate_na.reshape(shp), bias, rows_per_step=8)

    y = _merge(x2, a_f.reshape(n, F_WIDTH), a_na.reshape(n, NA_WIDTH), kv, g_norm, w_c, w_f, w_na, w_ca,
               w_out, g_final, seq, tm=512)
    return y.reshape(b, seq, d)


def kernel(x_prompt, x_sample, mem_prompt, mem_sample, g_norm, w_in, na_rpb, g_mem, w_mem_kv,
           w_f_out, w_na_out, w_ca_out, w_out, g_final):
    assert w_in.shape[0] == 1, "the fused final norm assumes a single layer"
    w_a = w_in[0, :, :A_WIDTH].astype(BF16)
    w_c = w_in[0, :, A_WIDTH:].astype(BF16)
    args = (g_norm[0][None], w_a, w_c, _na_bias(na_rpb[0]), g_mem[0][None], w_mem_kv[0].astype(BF16),
            w_f_out[0].astype(BF16), w_na_out[0].astype(BF16), w_ca_out[0].astype(BF16),
            w_out[0].astype(BF16), g_final[None], _shared_tables())
    return (_trunk(x_prompt, mem_prompt, *args), _trunk(x_sample, mem_sample, *args))
```

```python
import functools

import numpy as np
import jax
import jax.numpy as jnp
from jax import lax
from jax.experimental import pallas as pl
from jax.experimental.pallas import tpu as pltpu

D_MODEL = 1024
GRID_W = 64
N_MEM = 256
F_GROUPS = 4
F_GROUP_DIM = 96
F_WIDTH = F_GROUPS * F_GROUP_DIM
NA_HEADS = 6
HEAD_DIM = 64
NA_WIDTH = NA_HEADS * HEAD_DIM
CA_HEADS = 4
CA_WIDTH = CA_HEADS * HEAD_DIM
NA_KH = 8
NA_KW = 16
EPS = 1e-6
NEG = -1e30

LANES = 128
BF16_SUBLANES = 16
NA_ROWS_PER_GROUP = 4
ROW_PITCH = 72
A_WIDTH = 2 * F_WIDTH + 4 * NA_WIDTH
C_WIDTH = 2 * CA_WIDTH + 3 * D_MODEL
VMEM_LIMIT = 56 * 1024 * 1024

BF16 = jnp.bfloat16
F32 = jnp.float32


def _rmsnorm(x, g):
    return x * lax.rsqrt(jnp.mean(x * x, axis=-1, keepdims=True) + EPS) * g


def _dot(a, b):
    return jnp.dot(a, b, preferred_element_type=F32)


def _dot_nt(a, b):
    return lax.dot_general(a, b, (((1,), (1,)), ((), ())), preferred_element_type=F32)


def _params(*semantics):
    return pltpu.CompilerParams(dimension_semantics=semantics, vmem_limit_bytes=VMEM_LIMIT)


def _pair_attention(q_pair, k_pair, v_pair, bias):
    m = q_pair.shape[0]
    first = lax.broadcasted_iota(jnp.int32, (m, LANES), 1) < HEAD_DIM
    zero = jnp.zeros_like(q_pair)
    q2 = jnp.concatenate([jnp.where(first, q_pair, zero), jnp.where(first, zero, q_pair)], axis=0)
    s = _dot_nt(q2, k_pair)
    if bias is not None:
        s = s + bias
    p = jnp.exp(s - jnp.max(s, axis=-1, keepdims=True))
    l = jnp.sum(p, axis=-1, keepdims=True)
    o2 = _dot(p.astype(BF16), v_pair) / l
    return jnp.where(first, o2[:m], o2[m:])


def _mem_kv_kernel(mem_ref, g_ref, w_ref, kv_ref):
    h = _rmsnorm(mem_ref[0], g_ref[...]).astype(BF16)
    kv_ref[0] = _dot(h, w_ref[...]).astype(BF16)


def _mem_kv(mem, g_mem, w_kv):
    b = mem.shape[0]
    return pl.pallas_call(
        _mem_kv_kernel,
        grid=(b,),
        in_specs=[pl.BlockSpec((1, N_MEM, D_MODEL), lambda i: (i, 0, 0)),
                  pl.BlockSpec((1, D_MODEL), lambda i: (0, 0)),
                  pl.BlockSpec((D_MODEL, 2 * CA_WIDTH), lambda i: (0, 0))],
        out_specs=pl.BlockSpec((1, N_MEM, 2 * CA_WIDTH), lambda i: (i, 0, 0)),
        out_shape=jax.ShapeDtypeStruct((b, N_MEM, 2 * CA_WIDTH), BF16),
        compiler_params=_params("arbitrary"),
        name="mem_kv",
    )(mem, g_mem, w_kv)


def _in_proj_kernel(x_ref, g_ref, w_ref, cdft_ref, p_ref, q_ref, gf_ref, qn_ref, kn_ref, vn_ref, gn_ref,
                    t_ref, *, grid_rows):
    w = F_WIDTH
    slabs = 2 * w // LANES
    x = x_ref[...]
    rs = lax.rsqrt(jnp.mean(x * x, axis=-1, keepdims=True) + EPS)
    xb = (x * g_ref[...]).astype(BF16)
    zcols = lambda j: _dot(xb, w_ref[:, 2 * j * w:2 * (j + 1) * w])
    z_f = zcols(0)
    z_qk = zcols(1)
    pq = _dot(z_f[:, :w].astype(BF16), cdft_ref[...]) * rs
    for r in range(grid_rows):
        for s in range(slabs):
            t_ref[s, r * ROW_PITCH:r * ROW_PITCH + GRID_W, :] = pq[r * GRID_W:(r + 1) * GRID_W,
                                                                   s * LANES:(s + 1) * LANES]
    gf_ref[...] = (z_f[:, w:] * rs).astype(BF16)
    qn_ref[...] = (z_qk[:, :w] * (rs * (HEAD_DIM ** -0.5))).astype(BF16)
    kn_ref[...] = (z_qk[:, w:] * rs).astype(BF16)
    z_vg = zcols(2)
    vn_ref[...] = (z_vg[:, :w] * rs).astype(BF16)
    gn_ref[...] = (z_vg[:, w:] * rs).astype(BF16)
    for c in range(GRID_W):
        for s in range(slabs):
            v = t_ref[s, pl.ds(c, grid_rows, stride=ROW_PITCH), :].astype(BF16)
            dst = p_ref if s < slabs // 2 else q_ref
            lane0 = c * w + (s % (slabs // 2)) * LANES
            dst[0, :, lane0:lane0 + LANES] = v


def _in_proj(x2, g_norm, w_a, cdft, seq, tm):
    n = x2.shape[0]
    grid_rows = tm // GRID_W
    per_batch = seq // tm
    out = jax.ShapeDtypeStruct((n, F_WIDTH), BF16)
    ospec = pl.BlockSpec((tm, F_WIDTH), lambda i: (i, 0))
    wide = jax.ShapeDtypeStruct((n // seq, seq // GRID_W, GRID_W * F_WIDTH), BF16)
    wspec = pl.BlockSpec((1, grid_rows, GRID_W * F_WIDTH), lambda i: (i // per_batch, i % per_batch, 0))
    return pl.pallas_call(
        functools.partial(_in_proj_kernel, grid_rows=grid_rows),
        grid=(n // tm,),
        in_specs=[pl.BlockSpec((tm, D_MODEL), lambda i: (i, 0)),
                  pl.BlockSpec((1, D_MODEL), lambda i: (0, 0)),
                  pl.BlockSpec((D_MODEL, A_WIDTH), lambda i: (0, 0)),
                  pl.BlockSpec((F_WIDTH, 2 * F_WIDTH), lambda i: (0, 0))],
        out_specs=[wspec, wspec] + [ospec] * 5,
        out_shape=[wide, wide] + [out] * 5,
        scratch_shapes=[pltpu.VMEM((2 * F_WIDTH // LANES, grid_rows * ROW_PITCH, LANES), F32)],
        compiler_params=_params("arbitrary"),
        name="in_proj",
    )(x2, g_norm, w_a, cdft)


def _fft_rows_kernel(p_ref, q_ref, w_ref, ar_ref, ai_ref, *, rows, cols_per_step):
    for j in range(cols_per_step):
        sl = slice(j * F_WIDTH, (j + 1) * F_WIDTH)
        pq = jnp.concatenate([p_ref[0, :, sl], q_ref[0, :, sl]], axis=0)
        r = _dot(w_ref[j], pq)
        ar_ref[0, j] = r[:rows].astype(BF16)
        ai_ref[0, j] = r[rows:].astype(BF16)


def _fft_rows(p3, q3, w, rows, cols_per_step):
    b = p3.shape[0]
    blk = cols_per_step * F_WIDTH
    out = jax.ShapeDtypeStruct((b, GRID_W, rows, F_WIDTH), BF16)
    ospec = pl.BlockSpec((1, cols_per_step, rows, F_WIDTH), lambda j, i: (i, j, 0, 0))
    return pl.pallas_call(
        functools.partial(_fft_rows_kernel, rows=rows, cols_per_step=cols_per_step),
        grid=(GRID_W // cols_per_step, b),
        in_specs=[pl.BlockSpec((1, rows, blk), lambda j, i: (i, 0, j)),
                  pl.BlockSpec((1, rows, blk), lambda j, i: (i, 0, j)),
                  pl.BlockSpec((cols_per_step, 2 * rows, 2 * rows), lambda j, i: (j, 0, 0))],
        out_specs=[ospec, ospec],
        out_shape=[out, out],
        compiler_params=_params("arbitrary", "arbitrary"),
        name="fft_rows",
    )(p3, q3, w)


def _fft_cols_kernel(ar_ref, ai_ref, gate_ref, kc_ref, ks_ref, o_ref, *, norm, tiles):
    m = GRID_W * BF16_SUBLANES
    tile = lambda ref, t: ref[0, :, t * BF16_SUBLANES:(t + 1) * BF16_SUBLANES, :].reshape(m, F_WIDTH)
    xr = jnp.concatenate([tile(ar_ref, t) for t in range(tiles)], axis=1)
    xi = jnp.concatenate([tile(ai_ref, t) for t in range(tiles)], axis=1)
    y = _dot(kc_ref[...], xr) + _dot(ks_ref[...], xi)
    for t in range(tiles):
        g = tile(gate_ref, t).astype(F32)
        out = (y[:, t * F_WIDTH:(t + 1) * F_WIDTH] * norm * jax.nn.silu(g)).astype(BF16)
        o_ref[0, :, t * BF16_SUBLANES:(t + 1) * BF16_SUBLANES, :] = out.reshape(GRID_W, BF16_SUBLANES, F_WIDTH)


def _fft_cols(ar, ai, gate4, kc, ks, norm, tiles):
    b, _, rows, _ = ar.shape
    kb = tiles * BF16_SUBLANES
    m = GRID_W * BF16_SUBLANES
    spec = pl.BlockSpec((1, GRID_W, kb, F_WIDTH), lambda i, j: (i, 0, j, 0))
    wspec = pl.BlockSpec((m, m), lambda i, j: (0, 0))
    return pl.pallas_call(
        functools.partial(_fft_cols_kernel, norm=norm, tiles=tiles),
        grid=(b, rows // kb),
        in_specs=[spec, spec, spec, wspec, wspec],
        out_specs=spec,
        out_shape=jax.ShapeDtypeStruct((b, GRID_W, rows, F_WIDTH), BF16),
        compiler_params=_params("arbitrary", "arbitrary"),
        name="fft_cols",
    )(ar, ai, gate4, kc, ks)


def _na_kernel(q_ref, k_ref, v_ref, gate_ref, bias_ref, o_ref, *, rows, rows_per_step):
    i = pl.program_id(1)
    first = lax.broadcasted_iota(jnp.int32, (GRID_W, LANES), 1) < HEAD_DIM

    def group_body(t2, carry):
        chains = []
        for dt in range(NA_ROWS_PER_GROUP):
            t = t2 * NA_ROWS_PER_GROUP + dt
            r = i * rows_per_step + t
            rs = jnp.clip(r - NA_KH // 2, 0, rows - NA_KH)
            tok = pl.ds(pl.multiple_of(t * GRID_W, GRID_W), GRID_W)
            keys = pl.ds(pl.multiple_of(rs * GRID_W, GRID_W), NA_KH * GRID_W)
            for hp in range(NA_HEADS // 2):
                chains.append((tok, keys, slice(hp * LANES, (hp + 1) * LANES), hp, r - rs))
        scores = []
        for tok, keys, cs, hp, delta in chains:
            q_pair = q_ref[0, tok, cs]
            zero = jnp.zeros_like(q_pair)
            q2 = jnp.concatenate([jnp.where(first, q_pair, zero), jnp.where(first, zero, q_pair)], axis=0)
            scores.append(_dot_nt(q2, k_ref[0, keys, cs]) + bias_ref[hp, delta])
        probs = [jnp.exp(s - jnp.max(s, axis=-1, keepdims=True)) for s in scores]
        sums = [jnp.sum(p, axis=-1, keepdims=True) for p in probs]
        outs = [_dot(p.astype(BF16), v_ref[0, keys, cs]) / l
                for p, l, (tok, keys, cs, hp, delta) in zip(probs, sums, chains)]
        for o2, (tok, keys, cs, hp, delta) in zip(outs, chains):
            o = jnp.where(first, o2[:GRID_W], o2[GRID_W:])
            g = gate_ref[0, tok, cs].astype(F32)
            o_ref[0, tok, cs] = (o * jax.nn.silu(g)).astype(BF16)
        return carry

    lax.fori_loop(0, rows_per_step // NA_ROWS_PER_GROUP, group_body, 0)


def _na(q3, k3, v3, gate3, bias, rows_per_step):
    b, s, _ = q3.shape
    rows = s // GRID_W
    tm = rows_per_step * GRID_W
    tile = pl.BlockSpec((1, tm, NA_WIDTH), lambda i, j: (i, j, 0))
    full = pl.BlockSpec((1, s, NA_WIDTH), lambda i, j: (i, 0, 0))
    return pl.pallas_call(
        functools.partial(_na_kernel, rows=rows, rows_per_step=rows_per_step),
        grid=(b, rows // rows_per_step),
        in_specs=[tile, full, full, tile,
                  pl.BlockSpec(bias.shape, lambda i, j: (0, 0, 0, 0))],
        out_specs=tile,
        out_shape=jax.ShapeDtypeStruct((b, s, NA_WIDTH), BF16),
        compiler_params=_params("arbitrary", "arbitrary"),
        name="na_attn",
    )(q3, k3, v3, gate3, bias)


def _na_bias(rpb):
    c = np.arange(GRID_W)
    cs = np.clip(c - NA_KW // 2, 0, GRID_W - NA_KW)
    kc = np.arange(GRID_W)
    valid = (kc[None, :] >= cs[:, None]) & (kc[None, :] < cs[:, None] + NA_KW)
    dc = np.clip(kc[None, :] - c[:, None] + (NA_KW - 1), 0, 2 * NA_KW - 2)
    n_dr, n_dc = 2 * NA_KH - 1, 2 * NA_KW - 1
    onehot = (dc.reshape(-1)[None, :] == np.arange(n_dc)[:, None]).astype(np.float32)
    sel = jnp.dot(rpb.astype(F32).reshape(NA_HEADS * n_dr, n_dc), jnp.asarray(onehot),
                  precision=lax.Precision.HIGHEST)
    sel = sel.reshape(NA_HEADS, n_dr, GRID_W, GRID_W)
    sel = jnp.where(jnp.asarray(valid)[None, None], sel, NEG)
    per_delta = jnp.stack([sel[:, NA_KH - 1 - d:2 * NA_KH - 1 - d] for d in range(NA_KH)], axis=1)
    b = per_delta.transpose(0, 1, 3, 2, 4)
    b = b.reshape(NA_HEADS // 2, 2, NA_KH, GRID_W, NA_KH * GRID_W)
    return b.transpose(0, 2, 1, 3, 4).reshape(NA_HEADS // 2, NA_KH, 2 * GRID_W, NA_KH * GRID_W)


def _merge_kernel(x_ref, af_ref, ana_ref, km_ref, vm_ref, g_ref, wc_ref, wf_ref, wna_ref, wca_ref,
                  wout_ref, gfin_ref, y_ref, *, subtiles):
    tm = x_ref.shape[0] // subtiles
    m0 = 2 * CA_WIDTH
    for st in range(subtiles):
        rows = slice(st * tm, (st + 1) * tm)
        x = x_ref[rows, :]
        h = _rmsnorm(x, g_ref[...]).astype(BF16)
        gate = lambda j: jax.nn.sigmoid(_dot(h, wc_ref[:, m0 + j * D_MODEL:m0 + (j + 1) * D_MODEL]))
        merged = gate(0) * _dot(af_ref[rows, :], wf_ref[...])
        merged = merged + gate(1) * _dot(ana_ref[rows, :], wna_ref[...])
        z_ca = _dot(h, wc_ref[:, :m0])
        q_ca = (z_ca[:, :CA_WIDTH] * (HEAD_DIM ** -0.5)).astype(BF16)
        parts = []
        for hp in range(CA_HEADS // 2):
            cs = slice(hp * LANES, (hp + 1) * LANES)
            o = _pair_attention(q_ca[:, cs], km_ref[0, :, cs], vm_ref[0, :, cs], None)
            parts.append((o * jax.nn.silu(z_ca[:, CA_WIDTH + hp * LANES:CA_WIDTH + (hp + 1) * LANES])).astype(BF16))
        merged = merged + gate(2) * _dot(jnp.concatenate(parts, axis=1), wca_ref[...])
        out = x + _dot(merged.astype(BF16), wout_ref[...])
        y_ref[rows, :] = _rmsnorm(out, gfin_ref[...])


def _merge(x2, a_f, a_na, kv, g_norm, w_c, w_f, w_na, w_ca, w_out, g_final, seq, tm, subtiles):
    n = x2.shape[0]
    per_batch = seq // tm
    const = lambda i: (0, 0)
    once = dict(pipeline_mode=pl.Buffered(1))
    return pl.pallas_call(
        functools.partial(_merge_kernel, subtiles=subtiles),
        grid=(n // tm,),
        in_specs=[pl.BlockSpec((tm, D_MODEL), lambda i: (i, 0)),
                  pl.BlockSpec((tm, F_WIDTH), lambda i: (i, 0)),
                  pl.BlockSpec((tm, NA_WIDTH), lambda i: (i, 0)),
                  pl.BlockSpec((1, N_MEM, CA_WIDTH), lambda i: (i // per_batch, 0, 0)),
                  pl.BlockSpec((1, N_MEM, CA_WIDTH), lambda i: (i // per_batch, 0, 1)),
                  pl.BlockSpec((1, D_MODEL), const),
                  pl.BlockSpec((D_MODEL, C_WIDTH), const, **once),
                  pl.BlockSpec((F_WIDTH, D_MODEL), const, **once),
                  pl.BlockSpec((NA_WIDTH, D_MODEL), const, **once),
                  pl.BlockSpec((CA_WIDTH, D_MODEL), const, **once),
                  pl.BlockSpec((D_MODEL, D_MODEL), const, **once),
                  pl.BlockSpec((1, D_MODEL), const)],
        out_specs=pl.BlockSpec((tm, D_MODEL), lambda i: (i, 0)),
        out_shape=jax.ShapeDtypeStruct((n, D_MODEL), F32),
        compiler_params=_params("arbitrary"),
        name="merge_out",
    )(x2, a_f, a_na, kv, kv, g_norm, w_c, w_f, w_na, w_ca, w_out, g_final)


def _cos_sin(n):
    k = np.arange(n)
    ang = 2.0 * np.pi * ((k[:, None] * k[None, :]) % n) / n
    return np.cos(ang).astype(np.float32), np.sin(ang).astype(np.float32)


def _row_tables(rows):
    seq = rows * GRID_W
    shape = (GRID_W, rows, rows)
    col, k1, row = (lax.broadcasted_iota(jnp.int32, shape, d) for d in range(3))
    ang = ((k1 * (row * GRID_W + col)) % seq).astype(F32) * (2.0 * np.pi / seq)
    c, s = jnp.cos(ang), jnp.sin(ang)
    top = jnp.concatenate([c, -s], axis=2)
    bottom = jnp.concatenate([-s, -c], axis=2)
    return jnp.concatenate([top, bottom], axis=1).astype(BF16)


def _shared_tables():
    c, s = _cos_sin(F_GROUP_DIM)
    eye = np.eye(F_GROUPS, dtype=np.float32)
    cdft = jnp.asarray(np.concatenate([np.kron(eye, c), np.kron(eye, s)], axis=1)).astype(BF16)
    c, s = _cos_sin(GRID_W)
    eye = np.eye(BF16_SUBLANES, dtype=np.float32)
    return cdft, jnp.asarray(np.kron(c, eye)).astype(BF16), jnp.asarray(np.kron(s, eye)).astype(BF16)


def _trunk(x, mem, g_norm, w_a, w_c, bias, g_mem, w_kv, w_f, w_na, w_ca, w_out, g_final, tables):
    cdft, kc, ks = tables
    b, seq, d = x.shape
    rows = seq // GRID_W
    n = b * seq
    x2 = x.reshape(n, d)

    kv = _mem_kv(mem, g_mem, w_kv)
    p, q, gate_f, qn, kn, vn, gate_na = _in_proj(x2, g_norm, w_a, cdft, seq, tm=1024)

    ar, ai = _fft_rows(p, q, _row_tables(rows), rows, cols_per_step=16)
    a_f = _fft_cols(ar, ai, gate_f.reshape(b, GRID_W, rows, F_WIDTH), kc, ks,
                    norm=float((seq * F_GROUP_DIM) ** -0.5), tiles=2 if rows % (2 * BF16_SUBLANES) == 0 else 1)

    shp = (b, seq, NA_WIDTH)
    a_na = _na(qn.reshape(shp), kn.reshape(shp), vn.reshape(shp), gate_na.reshape(shp), bias, rows_per_step=8)

    y = _merge(x2, a_f.reshape(n, F_WIDTH), a_na.reshape(n, NA_WIDTH), kv, g_norm, w_c, w_f, w_na, w_ca,
               w_out, g_final, seq, tm=1024, subtiles=2)
    return y.reshape(b, seq, d)


def kernel(x_prompt, x_sample, mem_prompt, mem_sample, g_norm, w_in, na_rpb, g_mem, w_mem_kv,
           w_f_out, w_na_out, w_ca_out, w_out, g_final):
    assert w_in.shape[0] == 1, "the fused final norm assumes a single layer"
    w_a = w_in[0, :, :A_WIDTH].astype(BF16)
    w_c = w_in[0, :, A_WIDTH:].astype(BF16)
    args = (g_norm[0][None], w_a, w_c, _na_bias(na_rpb[0]), g_mem[0][None], w_mem_kv[0].astype(BF16),
            w_f_out[0].astype(BF16), w_na_out[0].astype(BF16), w_ca_out[0].astype(BF16),
            w_out[0].astype(BF16), g_final[None], _shared_tables())
    return (_trunk(x_prompt, mem_prompt, *args), _trunk(x_sample, mem_sample, *args))
```

```python
import functools

import numpy as np
import jax
import jax.numpy as jnp
from jax import lax
from jax.experimental import pallas as pl
from jax.experimental.pallas import tpu as pltpu

D_MODEL = 1024
GRID_W = 64
N_MEM = 256
F_GROUPS = 4
F_GROUP_DIM = 96
F_WIDTH = F_GROUPS * F_GROUP_DIM
NA_HEADS = 6
HEAD_DIM = 64
NA_WIDTH = NA_HEADS * HEAD_DIM
CA_HEADS = 4
CA_WIDTH = CA_HEADS * HEAD_DIM
NA_KH = 8
NA_KW = 16
EPS = 1e-6
NEG = -1e30

LANES = 128
BF16_SUBLANES = 16
NA_ROWS_PER_GROUP = 4
ROW_PITCH = 72
A_WIDTH = 2 * F_WIDTH + 4 * NA_WIDTH
C_WIDTH = 2 * CA_WIDTH + 3 * D_MODEL
VMEM_LIMIT = 56 * 1024 * 1024

BF16 = jnp.bfloat16
F32 = jnp.float32


def _rmsnorm(x, g):
    return x * lax.rsqrt(jnp.mean(x * x, axis=-1, keepdims=True) + EPS) * g


def _dot(a, b):
    return jnp.dot(a, b, preferred_element_type=F32)


def _dot_nt(a, b):
    return lax.dot_general(a, b, (((1,), (1,)), ((), ())), preferred_element_type=F32)


def _params(*semantics):
    return pltpu.CompilerParams(dimension_semantics=semantics, vmem_limit_bytes=VMEM_LIMIT)


def _pair_attention(q_pair, k_pair, v_pair, bias):
    m = q_pair.shape[0]
    first = lax.broadcasted_iota(jnp.int32, (m, LANES), 1) < HEAD_DIM
    zero = jnp.zeros_like(q_pair)
    q2 = jnp.concatenate([jnp.where(first, q_pair, zero), jnp.where(first, zero, q_pair)], axis=0)
    s = _dot_nt(q2, k_pair)
    if bias is not None:
        s = s + bias
    p = jnp.exp(s - jnp.max(s, axis=-1, keepdims=True))
    l = jnp.sum(p, axis=-1, keepdims=True)
    o2 = _dot(p.astype(BF16), v_pair) / l
    return jnp.where(first, o2[:m], o2[m:])


def _mem_kv_kernel(mem_ref, g_ref, w_ref, kv_ref):
    h = _rmsnorm(mem_ref[0], g_ref[...]).astype(BF16)
    kv_ref[0] = _dot(h, w_ref[...]).astype(BF16)


def _mem_kv(mem, g_mem, w_kv):
    b = mem.shape[0]
    return pl.pallas_call(
        _mem_kv_kernel,
        grid=(b,),
        in_specs=[pl.BlockSpec((1, N_MEM, D_MODEL), lambda i: (i, 0, 0)),
                  pl.BlockSpec((1, D_MODEL), lambda i: (0, 0)),
                  pl.BlockSpec((D_MODEL, 2 * CA_WIDTH), lambda i: (0, 0))],
        out_specs=pl.BlockSpec((1, N_MEM, 2 * CA_WIDTH), lambda i: (i, 0, 0)),
        out_shape=jax.ShapeDtypeStruct((b, N_MEM, 2 * CA_WIDTH), BF16),
        compiler_params=_params("arbitrary"),
        name="mem_kv",
    )(mem, g_mem, w_kv)


def _in_proj_kernel(x_ref, g_ref, w_ref, cdft_ref, p_ref, q_ref, gf_ref, qn_ref, kn_ref, vn_ref, gn_ref,
                    t_ref, *, grid_rows):
    w = F_WIDTH
    slabs = 2 * w // LANES
    x = x_ref[...]
    rs = lax.rsqrt(jnp.mean(x * x, axis=-1, keepdims=True) + EPS)
    xb = (x * g_ref[...]).astype(BF16)
    zcols = lambda j: _dot(xb, w_ref[:, 2 * j * w:2 * (j + 1) * w])
    z_f = zcols(0)
    z_qk = zcols(1)
    pq = _dot(z_f[:, :w].astype(BF16), cdft_ref[...]) * rs
    for r in range(grid_rows):
        for s in range(slabs):
            t_ref[s, r * ROW_PITCH:r * ROW_PITCH + GRID_W, :] = pq[r * GRID_W:(r + 1) * GRID_W,
                                                                   s * LANES:(s + 1) * LANES]
    gf_ref[...] = (z_f[:, w:] * rs).astype(BF16)
    qn_ref[...] = (z_qk[:, :w] * (rs * (HEAD_DIM ** -0.5))).astype(BF16)
    kn_ref[...] = (z_qk[:, w:] * rs).astype(BF16)
    z_vg = zcols(2)
    vn_ref[...] = (z_vg[:, :w] * rs).astype(BF16)
    gn_ref[...] = (z_vg[:, w:] * rs).astype(BF16)
    for c in range(GRID_W):
        for s in range(slabs):
            v = t_ref[s, pl.ds(c, grid_rows, stride=ROW_PITCH), :].astype(BF16)
            dst = p_ref if s < slabs // 2 else q_ref
            dst[0, s % (slabs // 2), :, c * LANES:(c + 1) * LANES] = v


def _in_proj(x2, g_norm, w_a, cdft, seq, tm):
    n = x2.shape[0]
    grid_rows = tm // GRID_W
    per_batch = seq // tm
    out = jax.ShapeDtypeStruct((n, F_WIDTH), BF16)
    ospec = pl.BlockSpec((tm, F_WIDTH), lambda i: (i, 0))
    blocks = F_WIDTH // LANES
    wide = jax.ShapeDtypeStruct((n // seq, blocks, seq // GRID_W, GRID_W * LANES), BF16)
    wspec = pl.BlockSpec((1, blocks, grid_rows, GRID_W * LANES), lambda i: (i // per_batch, 0, i % per_batch, 0))
    return pl.pallas_call(
        functools.partial(_in_proj_kernel, grid_rows=grid_rows),
        grid=(n // tm,),
        in_specs=[pl.BlockSpec((tm, D_MODEL), lambda i: (i, 0)),
                  pl.BlockSpec((1, D_MODEL), lambda i: (0, 0)),
                  pl.BlockSpec((D_MODEL, A_WIDTH), lambda i: (0, 0)),
                  pl.BlockSpec((F_WIDTH, 2 * F_WIDTH), lambda i: (0, 0))],
        out_specs=[wspec, wspec] + [ospec] * 5,
        out_shape=[wide, wide] + [out] * 5,
        scratch_shapes=[pltpu.VMEM((2 * F_WIDTH // LANES, grid_rows * ROW_PITCH, LANES), F32)],
        compiler_params=_params("arbitrary"),
        name="in_proj",
    )(x2, g_norm, w_a, cdft)


def _fft_kernel(p_ref, q_ref, gate_ref, w_ref, kc_ref, ks_ref, o_ref, ar_ref, ai_ref, *, rows, norm):
    for j in range(GRID_W):
        sl = slice(j * LANES, (j + 1) * LANES)
        pq = jnp.concatenate([p_ref[0, 0, :, sl], q_ref[0, 0, :, sl]], axis=0)
        r = _dot(w_ref[j], pq)
        ar_ref[j] = r[:rows].astype(BF16)
        ai_ref[j] = r[rows:].astype(BF16)
    m = GRID_W * BF16_SUBLANES
    tile = lambda ref, t: ref[:, t * BF16_SUBLANES:(t + 1) * BF16_SUBLANES, :].reshape(m, LANES)
    for tp in range(rows // (2 * BF16_SUBLANES)):
        xr = jnp.concatenate([tile(ar_ref, 2 * tp), tile(ar_ref, 2 * tp + 1)], axis=1)
        xi = jnp.concatenate([tile(ai_ref, 2 * tp), tile(ai_ref, 2 * tp + 1)], axis=1)
        y = _dot(kc_ref[...], xr) + _dot(ks_ref[...], xi)
        for half in range(2):
            k1 = slice((2 * tp + half) * BF16_SUBLANES, (2 * tp + half + 1) * BF16_SUBLANES)
            g = gate_ref[0, :, k1, :].reshape(m, LANES).astype(F32)
            out = (y[:, half * LANES:(half + 1) * LANES] * norm * jax.nn.silu(g)).astype(BF16)
            o_ref[0, :, k1, :] = out.reshape(GRID_W, BF16_SUBLANES, LANES)


def _fft(p4, q4, gate4, w, kc, ks, norm):
    b, blocks, rows, _ = p4.shape
    assert rows % (2 * BF16_SUBLANES) == 0, "the column DFT handles k1 in pairs of packed tiles"
    m = GRID_W * BF16_SUBLANES
    once = dict(pipeline_mode=pl.Buffered(1))
    in_spec = pl.BlockSpec((1, 1, rows, GRID_W * LANES), lambda i, c: (i, c, 0, 0))
    tok_spec = pl.BlockSpec((1, GRID_W, rows, LANES), lambda i, c: (i, 0, 0, c))
    return pl.pallas_call(
        functools.partial(_fft_kernel, rows=rows, norm=norm),
        grid=(b, blocks),
        in_specs=[in_spec, in_spec, tok_spec,
                  pl.BlockSpec((GRID_W, 2 * rows, 2 * rows), lambda i, c: (0, 0, 0), **once),
                  pl.BlockSpec((m, m), lambda i, c: (0, 0), **once),
                  pl.BlockSpec((m, m), lambda i, c: (0, 0), **once)],
        out_specs=tok_spec,
        out_shape=jax.ShapeDtypeStruct((b, GRID_W, rows, F_WIDTH), BF16),
        scratch_shapes=[pltpu.VMEM((GRID_W, rows, LANES), BF16), pltpu.VMEM((GRID_W, rows, LANES), BF16)],
        compiler_params=_params("arbitrary", "arbitrary"),
        name="fft",
    )(p4, q4, gate4, w, kc, ks)


def _na_kernel(q_ref, k_ref, v_ref, gate_ref, bias_ref, o_ref, *, rows, rows_per_step):
    i = pl.program_id(1)
    first = lax.broadcasted_iota(jnp.int32, (GRID_W, LANES), 1) < HEAD_DIM

    def group_body(t2, carry):
        chains = []
        for dt in range(NA_ROWS_PER_GROUP):
            t = t2 * NA_ROWS_PER_GROUP + dt
            r = i * rows_per_step + t
            rs = jnp.clip(r - NA_KH // 2, 0, rows - NA_KH)
            tok = pl.ds(pl.multiple_of(t * GRID_W, GRID_W), GRID_W)
            keys = pl.ds(pl.multiple_of(rs * GRID_W, GRID_W), NA_KH * GRID_W)
            for hp in range(NA_HEADS // 2):
                chains.append((tok, keys, slice(hp * LANES, (hp + 1) * LANES), hp, r - rs))
        scores = []
        for tok, keys, cs, hp, delta in chains:
            q_pair = q_ref[0, tok, cs]
            zero = jnp.zeros_like(q_pair)
            q2 = jnp.concatenate([jnp.where(first, q_pair, zero), jnp.where(first, zero, q_pair)], axis=0)
            scores.append(_dot_nt(q2, k_ref[0, keys, cs]) + bias_ref[hp, delta])
        probs = [jnp.exp(s - jnp.max(s, axis=-1, keepdims=True)) for s in scores]
        sums = [jnp.sum(p, axis=-1, keepdims=True) for p in probs]
        outs = [_dot(p.astype(BF16), v_ref[0, keys, cs]) / l
                for p, l, (tok, keys, cs, hp, delta) in zip(probs, sums, chains)]
        for o2, (tok, keys, cs, hp, delta) in zip(outs, chains):
            o = jnp.where(first, o2[:GRID_W], o2[GRID_W:])
            g = gate_ref[0, tok, cs].astype(F32)
            o_ref[0, tok, cs] = (o * jax.nn.silu(g)).astype(BF16)
        return carry

    lax.fori_loop(0, rows_per_step // NA_ROWS_PER_GROUP, group_body, 0)


def _na(q3, k3, v3, gate3, bias, rows_per_step):
    b, s, _ = q3.shape
    rows = s // GRID_W
    tm = rows_per_step * GRID_W
    tile = pl.BlockSpec((1, tm, NA_WIDTH), lambda i, j: (i, j, 0))
    full = pl.BlockSpec((1, s, NA_WIDTH), lambda i, j: (i, 0, 0))
    return pl.pallas_call(
        functools.partial(_na_kernel, rows=rows, rows_per_step=rows_per_step),
        grid=(b, rows // rows_per_step),
        in_specs=[tile, full, full, tile,
                  pl.BlockSpec(bias.shape, lambda i, j: (0, 0, 0, 0))],
        out_specs=tile,
        out_shape=jax.ShapeDtypeStruct((b, s, NA_WIDTH), BF16),
        compiler_params=_params("arbitrary", "arbitrary"),
        name="na_attn",
    )(q3, k3, v3, gate3, bias)


def _na_bias(rpb):
    c = np.arange(GRID_W)
    cs = np.clip(c - NA_KW // 2, 0, GRID_W - NA_KW)
    kc = np.arange(GRID_W)
    valid = (kc[None, :] >= cs[:, None]) & (kc[None, :] < cs[:, None] + NA_KW)
    dc = np.clip(kc[None, :] - c[:, None] + (NA_KW - 1), 0, 2 * NA_KW - 2)
    n_dr, n_dc = 2 * NA_KH - 1, 2 * NA_KW - 1
    onehot = (dc.reshape(-1)[None, :] == np.arange(n_dc)[:, None]).astype(np.float32)
    sel = jnp.dot(rpb.astype(F32).reshape(NA_HEADS * n_dr, n_dc), jnp.asarray(onehot),
                  precision=lax.Precision.HIGHEST)
    sel = sel.reshape(NA_HEADS, n_dr, GRID_W, GRID_W)
    sel = jnp.where(jnp.asarray(valid)[None, None], sel, NEG)
    per_delta = jnp.stack([sel[:, NA_KH - 1 - d:2 * NA_KH - 1 - d] for d in range(NA_KH)], axis=1)
    b = per_delta.transpose(0, 1, 3, 2, 4)
    b = b.reshape(NA_HEADS // 2, 2, NA_KH, GRID_W, NA_KH * GRID_W)
    return b.transpose(0, 2, 1, 3, 4).reshape(NA_HEADS // 2, NA_KH, 2 * GRID_W, NA_KH * GRID_W)


def _merge_kernel(x_ref, af_ref, ana_ref, km_ref, vm_ref, g_ref, wc_ref, wf_ref, wna_ref, wca_ref,
                  wout_ref, gfin_ref, y_ref, *, subtiles):
    tm = x_ref.shape[0] // subtiles
    m0 = 2 * CA_WIDTH
    for st in range(subtiles):
        rows = slice(st * tm, (st + 1) * tm)
        x = x_ref[rows, :]
        h = _rmsnorm(x, g_ref[...]).astype(BF16)
        gate = lambda j: jax.nn.sigmoid(_dot(h, wc_ref[:, m0 + j * D_MODEL:m0 + (j + 1) * D_MODEL]))
        merged = gate(0) * _dot(af_ref[rows, :], wf_ref[...])
        merged = merged + gate(1) * _dot(ana_ref[rows, :], wna_ref[...])
        z_ca = _dot(h, wc_ref[:, :m0])
        q_ca = (z_ca[:, :CA_WIDTH] * (HEAD_DIM ** -0.5)).astype(BF16)
        parts = []
        for hp in range(CA_HEADS // 2):
            cs = slice(hp * LANES, (hp + 1) * LANES)
            o = _pair_attention(q_ca[:, cs], km_ref[0, :, cs], vm_ref[0, :, cs], None)
            parts.append((o * jax.nn.silu(z_ca[:, CA_WIDTH + hp * LANES:CA_WIDTH + (hp + 1) * LANES])).astype(BF16))
        merged = merged + gate(2) * _dot(jnp.concatenate(parts, axis=1), wca_ref[...])
        out = x + _dot(merged.astype(BF16), wout_ref[...])
        y_ref[rows, :] = _rmsnorm(out, gfin_ref[...])


def _merge(x2, a_f, a_na, kv, g_norm, w_c, w_f, w_na, w_ca, w_out, g_final, seq, tm, subtiles):
    n = x2.shape[0]
    per_batch = seq // tm
    const = lambda i: (0, 0)
    once = dict(pipeline_mode=pl.Buffered(1))
    return pl.pallas_call(
        functools.partial(_merge_kernel, subtiles=subtiles),
        grid=(n // tm,),
        in_specs=[pl.BlockSpec((tm, D_MODEL), lambda i: (i, 0)),
                  pl.BlockSpec((tm, F_WIDTH), lambda i: (i, 0)),
                  pl.BlockSpec((tm, NA_WIDTH), lambda i: (i, 0)),
                  pl.BlockSpec((1, N_MEM, CA_WIDTH), lambda i: (i // per_batch, 0, 0)),
                  pl.BlockSpec((1, N_MEM, CA_WIDTH), lambda i: (i // per_batch, 0, 1)),
                  pl.BlockSpec((1, D_MODEL), const),
                  pl.BlockSpec((D_MODEL, C_WIDTH), const, **once),
                  pl.BlockSpec((F_WIDTH, D_MODEL), const, **once),
                  pl.BlockSpec((NA_WIDTH, D_MODEL), const, **once),
                  pl.BlockSpec((CA_WIDTH, D_MODEL), const, **once),
                  pl.BlockSpec((D_MODEL, D_MODEL), const, **once),
                  pl.BlockSpec((1, D_MODEL), const)],
        out_specs=pl.BlockSpec((tm, D_MODEL), lambda i: (i, 0)),
        out_shape=jax.ShapeDtypeStruct((n, D_MODEL), F32),
        compiler_params=_params("arbitrary"),
        name="merge_out",
    )(x2, a_f, a_na, kv, kv, g_norm, w_c, w_f, w_na, w_ca, w_out, g_final)


def _cos_sin(n):
    k = np.arange(n)
    ang = 2.0 * np.pi * ((k[:, None] * k[None, :]) % n) / n
    return np.cos(ang).astype(np.float32), np.sin(ang).astype(np.float32)


def _row_tables(rows):
    seq = rows * GRID_W
    cr, sr = (jnp.asarray(t)[None] for t in _cos_sin(rows))
    k1, col = np.arange(rows)[None, :, None], np.arange(GRID_W)[:, None, None]
    tw = 2.0 * np.pi * ((k1 * col) % seq) / seq
    ct, st = jnp.asarray(np.cos(tw), F32), jnp.asarray(np.sin(tw), F32)
    c, s = cr * ct - sr * st, sr * ct + cr * st
    top = jnp.concatenate([c, -s], axis=2)
    bottom = jnp.concatenate([-s, -c], axis=2)
    return jnp.concatenate([top, bottom], axis=1).astype(BF16)


def _shared_tables():
    c, s = _cos_sin(F_GROUP_DIM)
    eye = np.eye(F_GROUPS, dtype=np.float32)
    cdft = jnp.asarray(np.concatenate([np.kron(eye, c), np.kron(eye, s)], axis=1)).astype(BF16)
    c, s = _cos_sin(GRID_W)
    eye = np.eye(BF16_SUBLANES, dtype=np.float32)
    return cdft, jnp.asarray(np.kron(c, eye)).astype(BF16), jnp.asarray(np.kron(s, eye)).astype(BF16)


def _trunk(x, mem, g_norm, w_a, w_c, bias, g_mem, w_kv, w_f, w_na, w_ca, w_out, g_final, tables):
    cdft, kc, ks = tables
    b, seq, d = x.shape
    rows = seq // GRID_W
    n = b * seq
    x2 = x.reshape(n, d)

    kv = _mem_kv(mem, g_mem, w_kv)
    p, q, gate_f, qn, kn, vn, gate_na = _in_proj(x2, g_norm, w_a, cdft, seq, tm=1024)

    a_f = _fft(p, q, gate_f.reshape(b, GRID_W, rows, F_WIDTH), _row_tables(rows), kc, ks,
               norm=float((seq * F_GROUP_DIM) ** -0.5))

    shp = (b, seq, NA_WIDTH)
    a_na = _na(qn.reshape(shp), kn.reshape(shp), vn.reshape(shp), gate_na.reshape(shp), bias, rows_per_step=8)

    y = _merge(x2, a_f.reshape(n, F_WIDTH), a_na.reshape(n, NA_WIDTH), kv, g_norm, w_c, w_f, w_na, w_ca,
               w_out, g_final, seq, tm=1024, subtiles=2)
    return y.reshape(b, seq, d)


def kernel(x_prompt, x_sample, mem_prompt, mem_sample, g_norm, w_in, na_rpb, g_mem, w_mem_kv,
           w_f_out, w_na_out, w_ca_out, w_out, g_final):
    assert w_in.shape[0] == 1, "the fused final norm assumes a single layer"
    w_a = w_in[0, :, :A_WIDTH].astype(BF16)
    w_c = w_in[0, :, A_WIDTH:].astype(BF16)
    args = (g_norm[0][None], w_a, w_c, _na_bias(na_rpb[0]), g_mem[0][None], w_mem_kv[0].astype(BF16),
            w_f_out[0].astype(BF16), w_na_out[0].astype(BF16), w_ca_out[0].astype(BF16),
            w_out[0].astype(BF16), g_final[None], _shared_tables())
    return (_trunk(x_prompt, mem_prompt, *args), _trunk(x_sample, mem_sample, *args))
```

```python
import functools

import numpy as np
import jax
import jax.numpy as jnp
from jax import lax
from jax.experimental import pallas as pl
from jax.experimental.pallas import tpu as pltpu

D_MODEL = 1024
GRID_W = 64
N_MEM = 256
F_GROUPS = 4
F_GROUP_DIM = 96
F_WIDTH = F_GROUPS * F_GROUP_DIM
NA_HEADS = 6
HEAD_DIM = 64
NA_WIDTH = NA_HEADS * HEAD_DIM
CA_HEADS = 4
CA_WIDTH = CA_HEADS * HEAD_DIM
NA_KH = 8
NA_KW = 16
EPS = 1e-6
NEG = -1e30
LOG2E = float(np.log2(np.e))

LANES = 128
BF16_SUBLANES = 16
NA_ROWS_PER_GROUP = 8
ROW_PITCH = 72
A_WIDTH = 2 * F_WIDTH + 4 * NA_WIDTH
C_WIDTH = 2 * CA_WIDTH + 3 * D_MODEL
VMEM_LIMIT = 56 * 1024 * 1024

BF16 = jnp.bfloat16
F32 = jnp.float32


def _rmsnorm(x, g):
    return x * lax.rsqrt(jnp.mean(x * x, axis=-1, keepdims=True) + EPS) * g


def _dot(a, b):
    return jnp.dot(a, b, preferred_element_type=F32)


def _dot_nt(a, b):
    return lax.dot_general(a, b, (((1,), (1,)), ((), ())), preferred_element_type=F32)


def _params(*semantics):
    return pltpu.CompilerParams(dimension_semantics=semantics, vmem_limit_bytes=VMEM_LIMIT)


def _pair_attention(q_pair, k_pair, v_pair, bias):
    m = q_pair.shape[0]
    first = lax.broadcasted_iota(jnp.int32, (m, LANES), 1) < HEAD_DIM
    zero = jnp.zeros_like(q_pair)
    q2 = jnp.concatenate([jnp.where(first, q_pair, zero), jnp.where(first, zero, q_pair)], axis=0)
    s = _dot_nt(q2, k_pair)
    if bias is not None:
        s = s + bias
    p = jnp.exp(s - jnp.max(s, axis=-1, keepdims=True))
    l = jnp.sum(p, axis=-1, keepdims=True)
    o2 = _dot(p.astype(BF16), v_pair) / l
    return jnp.where(first, o2[:m], o2[m:])


def _mem_kv_kernel(mem_ref, g_ref, w_ref, kv_ref):
    nb = mem_ref.shape[0]
    h = _rmsnorm(mem_ref[...].reshape(nb * N_MEM, D_MODEL), g_ref[...]).astype(BF16)
    kv_ref[...] = _dot(h, w_ref[...]).astype(BF16).reshape(nb, N_MEM, 2 * CA_WIDTH)


def _mem_kv(mem, g_mem, w_kv):
    b = mem.shape[0]
    nb = int(np.gcd(b, 4))
    return pl.pallas_call(
        _mem_kv_kernel,
        grid=(b // nb,),
        in_specs=[pl.BlockSpec((nb, N_MEM, D_MODEL), lambda i: (i, 0, 0)),
                  pl.BlockSpec((1, D_MODEL), lambda i: (0, 0)),
                  pl.BlockSpec((D_MODEL, 2 * CA_WIDTH), lambda i: (0, 0))],
        out_specs=pl.BlockSpec((nb, N_MEM, 2 * CA_WIDTH), lambda i: (i, 0, 0)),
        out_shape=jax.ShapeDtypeStruct((b, N_MEM, 2 * CA_WIDTH), BF16),
        compiler_params=_params("arbitrary"),
        name="mem_kv",
    )(mem, g_mem, w_kv)


def _in_proj_kernel(x_ref, g_ref, w_ref, cdft_ref, p_ref, q_ref, gf_ref, qn_ref, kn_ref, vn_ref, gn_ref,
                    t_ref, *, grid_rows):
    w = F_WIDTH
    slabs = 2 * w // LANES
    x = x_ref[...]
    rs = lax.rsqrt(jnp.mean(x * x, axis=-1, keepdims=True) + EPS)
    xb = (x * g_ref[...]).astype(BF16)
    zcols = lambda j: _dot(xb, w_ref[:, 2 * j * w:2 * (j + 1) * w])
    z_f = zcols(0)
    z_qk = zcols(1)
    pq = _dot(z_f[:, :w].astype(BF16), cdft_ref[...]) * rs
    for r in range(grid_rows):
        for s in range(slabs):
            t_ref[s, r * ROW_PITCH:r * ROW_PITCH + GRID_W, :] = pq[r * GRID_W:(r + 1) * GRID_W,
                                                                   s * LANES:(s + 1) * LANES]
    gf_ref[...] = (z_f[:, w:] * rs).astype(BF16)
    qn_ref[...] = (z_qk[:, :w] * (rs * (HEAD_DIM ** -0.5 * LOG2E))).astype(BF16)
    kn_ref[...] = (z_qk[:, w:] * rs).astype(BF16)
    z_vg = zcols(2)
    vn_ref[...] = (z_vg[:, :w] * rs).astype(BF16)
    gn_ref[...] = (z_vg[:, w:] * rs).astype(BF16)
    for c in range(GRID_W):
        for s in range(slabs):
            v = t_ref[s, pl.ds(c, grid_rows, stride=ROW_PITCH), :].astype(BF16)
            dst = p_ref if s < slabs // 2 else q_ref
            dst[0, s % (slabs // 2), :, c * LANES:(c + 1) * LANES] = v


def _in_proj(x2, g_norm, w_a, cdft, seq, tm):
    n = x2.shape[0]
    grid_rows = tm // GRID_W
    per_batch = seq // tm
    out = jax.ShapeDtypeStruct((n, F_WIDTH), BF16)
    ospec = pl.BlockSpec((tm, F_WIDTH), lambda i: (i, 0))
    blocks = F_WIDTH // LANES
    wide = jax.ShapeDtypeStruct((n // seq, blocks, seq // GRID_W, GRID_W * LANES), BF16)
    wspec = pl.BlockSpec((1, blocks, grid_rows, GRID_W * LANES), lambda i: (i // per_batch, 0, i % per_batch, 0))
    return pl.pallas_call(
        functools.partial(_in_proj_kernel, grid_rows=grid_rows),
        grid=(n // tm,),
        in_specs=[pl.BlockSpec((tm, D_MODEL), lambda i: (i, 0)),
                  pl.BlockSpec((1, D_MODEL), lambda i: (0, 0)),
                  pl.BlockSpec((D_MODEL, A_WIDTH), lambda i: (0, 0)),
                  pl.BlockSpec((F_WIDTH, 2 * F_WIDTH), lambda i: (0, 0))],
        out_specs=[wspec, wspec] + [ospec] * 5,
        out_shape=[wide, wide] + [out] * 5,
        scratch_shapes=[pltpu.VMEM((2 * F_WIDTH // LANES, grid_rows * ROW_PITCH, LANES), F32)],
        compiler_params=_params("arbitrary"),
        name="in_proj",
    )(x2, g_norm, w_a, cdft)


def _fft_kernel(p_ref, q_ref, gate_ref, w_ref, kc_ref, ks_ref, o_ref, ar_ref, ai_ref, *, rows, norm):
    for j in range(GRID_W):
        sl = slice(j * LANES, (j + 1) * LANES)
        pq = jnp.concatenate([p_ref[0, 0, :, sl], q_ref[0, 0, :, sl]], axis=0)
        r = _dot(w_ref[j], pq)
        ar_ref[j] = r[:rows].astype(BF16)
        ai_ref[j] = r[rows:].astype(BF16)
    m = GRID_W * BF16_SUBLANES
    tile = lambda ref, t: ref[:, t * BF16_SUBLANES:(t + 1) * BF16_SUBLANES, :].reshape(m, LANES)
    for tp in range(rows // (2 * BF16_SUBLANES)):
        xr = jnp.concatenate([tile(ar_ref, 2 * tp), tile(ar_ref, 2 * tp + 1)], axis=1)
        xi = jnp.concatenate([tile(ai_ref, 2 * tp), tile(ai_ref, 2 * tp + 1)], axis=1)
        y = _dot(kc_ref[...], xr) + _dot(ks_ref[...], xi)
        for half in range(2):
            k1 = slice((2 * tp + half) * BF16_SUBLANES, (2 * tp + half + 1) * BF16_SUBLANES)
            g = gate_ref[0, :, k1, :].reshape(m, LANES).astype(F32)
            out = (y[:, half * LANES:(half + 1) * LANES] * norm * jax.nn.silu(g)).astype(BF16)
            o_ref[0, :, k1, :] = out.reshape(GRID_W, BF16_SUBLANES, LANES)


def _fft(p4, q4, gate4, w, kc, ks, norm):
    b, blocks, rows, _ = p4.shape
    assert rows % (2 * BF16_SUBLANES) == 0, "the column DFT handles k1 in pairs of packed tiles"
    m = GRID_W * BF16_SUBLANES
    once = dict(pipeline_mode=pl.Buffered(1))
    in_spec = pl.BlockSpec((1, 1, rows, GRID_W * LANES), lambda i, c: (i, c, 0, 0))
    tok_spec = pl.BlockSpec((1, GRID_W, rows, LANES), lambda i, c: (i, 0, 0, c))
    return pl.pallas_call(
        functools.partial(_fft_kernel, rows=rows, norm=norm),
        grid=(b, blocks),
        in_specs=[in_spec, in_spec, tok_spec,
                  pl.BlockSpec((GRID_W, 2 * rows, 2 * rows), lambda i, c: (0, 0, 0), **once),
                  pl.BlockSpec((m, m), lambda i, c: (0, 0), **once),
                  pl.BlockSpec((m, m), lambda i, c: (0, 0), **once)],
        out_specs=tok_spec,
        out_shape=jax.ShapeDtypeStruct((b, GRID_W, rows, F_WIDTH), BF16),
        scratch_shapes=[pltpu.VMEM((GRID_W, rows, LANES), BF16), pltpu.VMEM((GRID_W, rows, LANES), BF16)],
        compiler_params=_params("arbitrary", "arbitrary"),
        name="fft",
    )(p4, q4, gate4, w, kc, ks)


def _na_kernel(q_ref, k_ref, v_ref, gate_ref, bias_ref, o_ref, *, rows, rows_per_step):
    i = pl.program_id(1)
    first = lax.broadcasted_iota(jnp.int32, (GRID_W, LANES), 1) < HEAD_DIM

    def group_body(t2, carry):
        chains = []
        for dt in range(NA_ROWS_PER_GROUP):
            t = t2 * NA_ROWS_PER_GROUP + dt
            r = i * rows_per_step + t
            rs = jnp.clip(r - NA_KH // 2, 0, rows - NA_KH)
            tok = pl.ds(pl.multiple_of(t * GRID_W, GRID_W), GRID_W)
            keys = pl.ds(pl.multiple_of(rs * GRID_W, GRID_W), NA_KH * GRID_W)
            for hp in range(NA_HEADS // 2):
                chains.append((tok, keys, slice(hp * LANES, (hp + 1) * LANES), hp, r - rs))
        scores = []
        for tok, keys, cs, hp, delta in chains:
            q_pair = q_ref[0, tok, cs]
            zero = jnp.zeros_like(q_pair)
            q2 = jnp.concatenate([jnp.where(first, q_pair, zero), jnp.where(first, zero, q_pair)], axis=0)
            scores.append(_dot_nt(q2, k_ref[0, keys, cs]) + bias_ref[hp, delta])
        probs = [jnp.exp2(s - jnp.max(s, axis=-1, keepdims=True)) for s in scores]
        sums = [jnp.sum(p, axis=-1, keepdims=True) for p in probs]
        outs = [_dot(p.astype(BF16), v_ref[0, keys, cs]) / l
                for p, l, (tok, keys, cs, hp, delta) in zip(probs, sums, chains)]
        for o2, (tok, keys, cs, hp, delta) in zip(outs, chains):
            o = jnp.where(first, o2[:GRID_W], o2[GRID_W:])
            g = gate_ref[0, tok, cs].astype(F32)
            o_ref[0, tok, cs] = (o * jax.nn.silu(g)).astype(BF16)
        return carry

    lax.fori_loop(0, rows_per_step // NA_ROWS_PER_GROUP, group_body, 0)


def _na(q3, k3, v3, gate3, bias, rows_per_step):
    b, s, _ = q3.shape
    rows = s // GRID_W
    tm = rows_per_step * GRID_W
    tile = pl.BlockSpec((1, tm, NA_WIDTH), lambda i, j: (i, j, 0))
    full = pl.BlockSpec((1, s, NA_WIDTH), lambda i, j: (i, 0, 0))
    return pl.pallas_call(
        functools.partial(_na_kernel, rows=rows, rows_per_step=rows_per_step),
        grid=(b, rows // rows_per_step),
        in_specs=[tile, full, full, tile,
                  pl.BlockSpec(bias.shape, lambda i, j: (0, 0, 0, 0))],
        out_specs=tile,
        out_shape=jax.ShapeDtypeStruct((b, s, NA_WIDTH), BF16),
        compiler_params=_params("arbitrary", "arbitrary"),
        name="na_attn",
    )(q3, k3, v3, gate3, bias)


def _na_bias(rpb):
    c = np.arange(GRID_W)
    cs = np.clip(c - NA_KW // 2, 0, GRID_W - NA_KW)
    kc = np.arange(GRID_W)
    valid = (kc[None, :] >= cs[:, None]) & (kc[None, :] < cs[:, None] + NA_KW)
    dc = np.clip(kc[None, :] - c[:, None] + (NA_KW - 1), 0, 2 * NA_KW - 2)
    n_dr, n_dc = 2 * NA_KH - 1, 2 * NA_KW - 1
    onehot = (dc.reshape(-1)[None, :] == np.arange(n_dc)[:, None]).astype(np.float32)
    sel = jnp.dot(rpb.astype(F32).reshape(NA_HEADS * n_dr, n_dc), jnp.asarray(onehot),
                  precision=lax.Precision.HIGHEST)
    sel = sel.reshape(NA_HEADS, n_dr, GRID_W, GRID_W)
    sel = jnp.where(jnp.asarray(valid)[None, None], sel * LOG2E, NEG)
    per_delta = jnp.stack([sel[:, NA_KH - 1 - d:2 * NA_KH - 1 - d] for d in range(NA_KH)], axis=1)
    b = per_delta.transpose(0, 1, 3, 2, 4)
    b = b.reshape(NA_HEADS // 2, 2, NA_KH, GRID_W, NA_KH * GRID_W)
    return b.transpose(0, 2, 1, 3, 4).reshape(NA_HEADS // 2, NA_KH, 2 * GRID_W, NA_KH * GRID_W)


def _merge_kernel(x_ref, af_ref, ana_ref, km_ref, vm_ref, g_ref, wc_ref, wf_ref, wna_ref, wca_ref,
                  wout_ref, gfin_ref, y_ref, *, subtiles):
    tm = x_ref.shape[0] // subtiles
    m0 = 2 * CA_WIDTH
    for st in range(subtiles):
        rows = slice(st * tm, (st + 1) * tm)
        x = x_ref[rows, :]
        h = _rmsnorm(x, g_ref[...]).astype(BF16)
        gate = lambda j: jax.nn.sigmoid(_dot(h, wc_ref[:, m0 + j * D_MODEL:m0 + (j + 1) * D_MODEL]))
        merged = gate(0) * _dot(af_ref[rows, :], wf_ref[...])
        merged = merged + gate(1) * _dot(ana_ref[rows, :], wna_ref[...])
        z_ca = _dot(h, wc_ref[:, :m0])
        q_ca = (z_ca[:, :CA_WIDTH] * (HEAD_DIM ** -0.5)).astype(BF16)
        parts = []
        for hp in range(CA_HEADS // 2):
            cs = slice(hp * LANES, (hp + 1) * LANES)
            o = _pair_attention(q_ca[:, cs], km_ref[0, :, cs], vm_ref[0, :, cs], None)
            parts.append((o * jax.nn.silu(z_ca[:, CA_WIDTH + hp * LANES:CA_WIDTH + (hp + 1) * LANES])).astype(BF16))
        merged = merged + gate(2) * _dot(jnp.concatenate(parts, axis=1), wca_ref[...])
        out = x + _dot(merged.astype(BF16), wout_ref[...])
        y_ref[rows, :] = _rmsnorm(out, gfin_ref[...])


def _merge(x2, a_f, a_na, kv, g_norm, w_c, w_f, w_na, w_ca, w_out, g_final, seq, tm, subtiles):
    n = x2.shape[0]
    per_batch = seq // tm
    const = lambda i: (0, 0)
    once = dict(pipeline_mode=pl.Buffered(1))
    return pl.pallas_call(
        functools.partial(_merge_kernel, subtiles=subtiles),
        grid=(n // tm,),
        in_specs=[pl.BlockSpec((tm, D_MODEL), lambda i: (i, 0)),
                  pl.BlockSpec((tm, F_WIDTH), lambda i: (i, 0)),
                  pl.BlockSpec((tm, NA_WIDTH), lambda i: (i, 0)),
                  pl.BlockSpec((1, N_MEM, CA_WIDTH), lambda i: (i // per_batch, 0, 0)),
                  pl.BlockSpec((1, N_MEM, CA_WIDTH), lambda i: (i // per_batch, 0, 1)),
                  pl.BlockSpec((1, D_MODEL), const),
                  pl.BlockSpec((D_MODEL, C_WIDTH), const, **once),
                  pl.BlockSpec((F_WIDTH, D_MODEL), const, **once),
                  pl.BlockSpec((NA_WIDTH, D_MODEL), const, **once),
                  pl.BlockSpec((CA_WIDTH, D_MODEL), const, **once),
                  pl.BlockSpec((D_MODEL, D_MODEL), const, **once),
                  pl.BlockSpec((1, D_MODEL), const)],
        out_specs=pl.BlockSpec((tm, D_MODEL), lambda i: (i, 0)),
        out_shape=jax.ShapeDtypeStruct((n, D_MODEL), F32),
        compiler_params=_params("arbitrary"),
        name="merge_out",
    )(x2, a_f, a_na, kv, kv, g_norm, w_c, w_f, w_na, w_ca, w_out, g_final)


def _cos_sin(n):
    k = np.arange(n)
    ang = 2.0 * np.pi * ((k[:, None] * k[None, :]) % n) / n
    return np.cos(ang).astype(np.float32), np.sin(ang).astype(np.float32)


def _row_tables(rows):
    seq = rows * GRID_W
    cr, sr = (jnp.asarray(t)[None] for t in _cos_sin(rows))
    k1, col = np.arange(rows)[None, :, None], np.arange(GRID_W)[:, None, None]
    tw = 2.0 * np.pi * ((k1 * col) % seq) / seq
    ct, st = jnp.asarray(np.cos(tw), F32), jnp.asarray(np.sin(tw), F32)
    c, s = cr * ct - sr * st, sr * ct + cr * st
    top = jnp.concatenate([c, -s], axis=2)
    bottom = jnp.concatenate([-s, -c], axis=2)
    return jnp.concatenate([top, bottom], axis=1).astype(BF16)


def _shared_tables():
    c, s = _cos_sin(F_GROUP_DIM)
    eye = np.eye(F_GROUPS, dtype=np.float32)
    cdft = jnp.asarray(np.concatenate([np.kron(eye, c), np.kron(eye, s)], axis=1)).astype(BF16)
    c, s = _cos_sin(GRID_W)
    eye = np.eye(BF16_SUBLANES, dtype=np.float32)
    return cdft, jnp.asarray(np.kron(c, eye)).astype(BF16), jnp.asarray(np.kron(s, eye)).astype(BF16)


def _trunk(x, mem, g_norm, w_a, w_c, bias, g_mem, w_kv, w_f, w_na, w_ca, w_out, g_final, tables):
    cdft, kc, ks = tables
    b, seq, d = x.shape
    rows = seq // GRID_W
    n = b * seq
    x2 = x.reshape(n, d)

    kv = _mem_kv(mem, g_mem, w_kv)
    p, q, gate_f, qn, kn, vn, gate_na = _in_proj(x2, g_norm, w_a, cdft, seq, tm=1024)

    a_f = _fft(p, q, gate_f.reshape(b, GRID_W, rows, F_WIDTH), _row_tables(rows), kc, ks,
               norm=float((seq * F_GROUP_DIM) ** -0.5))

    shp = (b, seq, NA_WIDTH)
    a_na = _na(qn.reshape(shp), kn.reshape(shp), vn.reshape(shp), gate_na.reshape(shp), bias, rows_per_step=8)

    y = _merge(x2, a_f.reshape(n, F_WIDTH), a_na.reshape(n, NA_WIDTH), kv, g_norm, w_c, w_f, w_na, w_ca,
               w_out, g_final, seq, tm=1024, subtiles=2)
    return y.reshape(b, seq, d)


def kernel(x_prompt, x_sample, mem_prompt, mem_sample, g_norm, w_in, na_rpb, g_mem, w_mem_kv,
           w_f_out, w_na_out, w_ca_out, w_out, g_final):
    assert w_in.shape[0] == 1, "the fused final norm assumes a single layer"
    w_a = w_in[0, :, :A_WIDTH].astype(BF16)
    w_c = w_in[0, :, A_WIDTH:].astype(BF16)
    args = (g_norm[0][None], w_a, w_c, _na_bias(na_rpb[0]), g_mem[0][None], w_mem_kv[0].astype(BF16),
            w_f_out[0].astype(BF16), w_na_out[0].astype(BF16), w_ca_out[0].astype(BF16),
            w_out[0].astype(BF16), g_final[None], _shared_tables())
    return (_trunk(x_prompt, mem_prompt, *args), _trunk(x_sample, mem_sample, *args))
```

```python
import functools

import numpy as np
import jax
import jax.numpy as jnp
from jax import lax
from jax.experimental import pallas as pl
from jax.experimental.pallas import tpu as pltpu

D_MODEL = 1024
GRID_W = 64
N_MEM = 256
F_GROUPS = 4
F_GROUP_DIM = 96
F_WIDTH = F_GROUPS * F_GROUP_DIM
NA_HEADS = 6
HEAD_DIM = 64
NA_WIDTH = NA_HEADS * HEAD_DIM
CA_HEADS = 4
CA_WIDTH = CA_HEADS * HEAD_DIM
NA_KH = 8
NA_KW = 16
NA_QBLK = 16
NA_CBLKS = GRID_W // NA_QBLK
EPS = 1e-6
NEG = -1e30
LOG2E = float(np.log2(np.e))

LANES = 128
BF16_SUBLANES = 16
NA_ROWS_PER_GROUP = 8
ROW_PITCH = 72
A_WIDTH = 2 * F_WIDTH + 4 * NA_WIDTH
C_WIDTH = 2 * CA_WIDTH + 3 * D_MODEL
VMEM_LIMIT = 56 * 1024 * 1024

BF16 = jnp.bfloat16
F32 = jnp.float32


def _rmsnorm(x, g):
    return x * lax.rsqrt(jnp.mean(x * x, axis=-1, keepdims=True) + EPS) * g


def _dot(a, b):
    return jnp.dot(a, b, preferred_element_type=F32)


def _dot_nt(a, b):
    return lax.dot_general(a, b, (((1,), (1,)), ((), ())), preferred_element_type=F32)


def _params(*semantics):
    return pltpu.CompilerParams(dimension_semantics=semantics, vmem_limit_bytes=VMEM_LIMIT)


def _pair_attention(q_pair, k_pair, v_pair, bias):
    m = q_pair.shape[0]
    first = lax.broadcasted_iota(jnp.int32, (m, LANES), 1) < HEAD_DIM
    zero = jnp.zeros_like(q_pair)
    q2 = jnp.concatenate([jnp.where(first, q_pair, zero), jnp.where(first, zero, q_pair)], axis=0)
    s = _dot_nt(q2, k_pair)
    if bias is not None:
        s = s + bias
    p = jnp.exp(s - jnp.max(s, axis=-1, keepdims=True))
    l = jnp.sum(p, axis=-1, keepdims=True)
    o2 = _dot(p.astype(BF16), v_pair) / l
    return jnp.where(first, o2[:m], o2[m:])


def _mem_kv_kernel(mem_ref, g_ref, w_ref, kv_ref):
    nb = mem_ref.shape[0]
    h = _rmsnorm(mem_ref[...].reshape(nb * N_MEM, D_MODEL), g_ref[...]).astype(BF16)
    kv_ref[...] = _dot(h, w_ref[...]).astype(BF16).reshape(nb, N_MEM, 2 * CA_WIDTH)


def _mem_kv(mem, g_mem, w_kv):
    b = mem.shape[0]
    nb = int(np.gcd(b, 4))
    return pl.pallas_call(
        _mem_kv_kernel,
        grid=(b // nb,),
        in_specs=[pl.BlockSpec((nb, N_MEM, D_MODEL), lambda i: (i, 0, 0)),
                  pl.BlockSpec((1, D_MODEL), lambda i: (0, 0)),
                  pl.BlockSpec((D_MODEL, 2 * CA_WIDTH), lambda i: (0, 0))],
        out_specs=pl.BlockSpec((nb, N_MEM, 2 * CA_WIDTH), lambda i: (i, 0, 0)),
        out_shape=jax.ShapeDtypeStruct((b, N_MEM, 2 * CA_WIDTH), BF16),
        compiler_params=_params("arbitrary"),
        name="mem_kv",
    )(mem, g_mem, w_kv)


def _store_column_slabs(ref, tokens, grid_rows):
    for r in range(grid_rows):
        for cb in range(NA_CBLKS):
            t0 = r * GRID_W + cb * NA_QBLK
            ref[0, cb, r * NA_QBLK:(r + 1) * NA_QBLK, :] = tokens[t0:t0 + NA_QBLK]


def _in_proj_kernel(x_ref, g_ref, w_ref, cdft_ref, p_ref, q_ref, gf_ref, qn_ref, kn_ref, vn_ref, gn_ref,
                    t_ref, *, grid_rows):
    w = F_WIDTH
    slabs = 2 * w // LANES
    x = x_ref[...]
    rs = lax.rsqrt(jnp.mean(x * x, axis=-1, keepdims=True) + EPS)
    xb = (x * g_ref[...]).astype(BF16)
    zcols = lambda j: _dot(xb, w_ref[:, 2 * j * w:2 * (j + 1) * w])
    z_f = zcols(0)
    z_qk = zcols(1)
    pq = _dot(z_f[:, :w].astype(BF16), cdft_ref[...]) * rs
    for r in range(grid_rows):
        for s in range(slabs):
            t_ref[s, r * ROW_PITCH:r * ROW_PITCH + GRID_W, :] = pq[r * GRID_W:(r + 1) * GRID_W,
                                                                   s * LANES:(s + 1) * LANES]
    gf_ref[...] = (z_f[:, w:] * rs).astype(BF16)
    qn_ref[...] = (z_qk[:, :w] * (rs * (HEAD_DIM ** -0.5 * LOG2E))).astype(BF16)
    _store_column_slabs(kn_ref, (z_qk[:, w:] * rs).astype(BF16), grid_rows)
    z_vg = zcols(2)
    _store_column_slabs(vn_ref, (z_vg[:, :w] * rs).astype(BF16), grid_rows)
    gn_ref[...] = (z_vg[:, w:] * rs).astype(BF16)
    for c in range(GRID_W):
        for s in range(slabs):
            v = t_ref[s, pl.ds(c, grid_rows, stride=ROW_PITCH), :].astype(BF16)
            dst = p_ref if s < slabs // 2 else q_ref
            dst[0, s % (slabs // 2), :, c * LANES:(c + 1) * LANES] = v


def _in_proj(x2, g_norm, w_a, cdft, seq, tm):
    n = x2.shape[0]
    grid_rows = tm // GRID_W
    per_batch = seq // tm
    out = jax.ShapeDtypeStruct((n, F_WIDTH), BF16)
    ospec = pl.BlockSpec((tm, F_WIDTH), lambda i: (i, 0))
    blocks = F_WIDTH // LANES
    wide = jax.ShapeDtypeStruct((n // seq, blocks, seq // GRID_W, GRID_W * LANES), BF16)
    wspec = pl.BlockSpec((1, blocks, grid_rows, GRID_W * LANES), lambda i: (i // per_batch, 0, i % per_batch, 0))
    slabs = jax.ShapeDtypeStruct((n // seq, NA_CBLKS, seq // NA_CBLKS, NA_WIDTH), BF16)
    sspec = pl.BlockSpec((1, NA_CBLKS, grid_rows * NA_QBLK, NA_WIDTH), lambda i: (i // per_batch, 0, i % per_batch, 0))
    return pl.pallas_call(
        functools.partial(_in_proj_kernel, grid_rows=grid_rows),
        grid=(n // tm,),
        in_specs=[pl.BlockSpec((tm, D_MODEL), lambda i: (i, 0)),
                  pl.BlockSpec((1, D_MODEL), lambda i: (0, 0)),
                  pl.BlockSpec((D_MODEL, A_WIDTH), lambda i: (0, 0)),
                  pl.BlockSpec((F_WIDTH, 2 * F_WIDTH), lambda i: (0, 0))],
        out_specs=[wspec, wspec, ospec, ospec, sspec, sspec, ospec],
        out_shape=[wide, wide, out, out, slabs, slabs, out],
        scratch_shapes=[pltpu.VMEM((2 * F_WIDTH // LANES, grid_rows * ROW_PITCH, LANES), F32)],
        compiler_params=_params("arbitrary"),
        name="in_proj",
    )(x2, g_norm, w_a, cdft)


def _fft_kernel(p_ref, q_ref, gate_ref, w_ref, kc_ref, ks_ref, o_ref, ar_ref, ai_ref, *, rows, norm):
    for j in range(GRID_W):
        sl = slice(j * LANES, (j + 1) * LANES)
        pq = jnp.concatenate([p_ref[0, 0, :, sl], q_ref[0, 0, :, sl]], axis=0)
        r = _dot(w_ref[j], pq)
        ar_ref[j] = r[:rows].astype(BF16)
        ai_ref[j] = r[rows:].astype(BF16)
    m = GRID_W * BF16_SUBLANES
    tile = lambda ref, t: ref[:, t * BF16_SUBLANES:(t + 1) * BF16_SUBLANES, :].reshape(m, LANES)
    for tp in range(rows // (2 * BF16_SUBLANES)):
        xr = jnp.concatenate([tile(ar_ref, 2 * tp), tile(ar_ref, 2 * tp + 1)], axis=1)
        xi = jnp.concatenate([tile(ai_ref, 2 * tp), tile(ai_ref, 2 * tp + 1)], axis=1)
        y = _dot(kc_ref[...], xr) + _dot(ks_ref[...], xi)
        for half in range(2):
            k1 = slice((2 * tp + half) * BF16_SUBLANES, (2 * tp + half + 1) * BF16_SUBLANES)
            g = gate_ref[0, :, k1, :].reshape(m, LANES).astype(F32)
            out = (y[:, half * LANES:(half + 1) * LANES] * norm * jax.nn.silu(g)).astype(BF16)
            o_ref[0, :, k1, :] = out.reshape(GRID_W, BF16_SUBLANES, LANES)


def _fft(p4, q4, gate4, w, kc, ks, norm):
    b, blocks, rows, _ = p4.shape
    assert rows % (2 * BF16_SUBLANES) == 0, "the column DFT handles k1 in pairs of packed tiles"
    m = GRID_W * BF16_SUBLANES
    once = dict(pipeline_mode=pl.Buffered(1))
    in_spec = pl.BlockSpec((1, 1, rows, GRID_W * LANES), lambda i, c: (i, c, 0, 0))
    tok_spec = pl.BlockSpec((1, GRID_W, rows, LANES), lambda i, c: (i, 0, 0, c))
    return pl.pallas_call(
        functools.partial(_fft_kernel, rows=rows, norm=norm),
        grid=(b, blocks),
        in_specs=[in_spec, in_spec, tok_spec,
                  pl.BlockSpec((GRID_W, 2 * rows, 2 * rows), lambda i, c: (0, 0, 0), **once),
                  pl.BlockSpec((m, m), lambda i, c: (0, 0), **once),
                  pl.BlockSpec((m, m), lambda i, c: (0, 0), **once)],
        out_specs=tok_spec,
        out_shape=jax.ShapeDtypeStruct((b, GRID_W, rows, F_WIDTH), BF16),
        scratch_shapes=[pltpu.VMEM((GRID_W, rows, LANES), BF16), pltpu.VMEM((GRID_W, rows, LANES), BF16)],
        compiler_params=_params("arbitrary", "arbitrary"),
        name="fft",
    )(p4, q4, gate4, w, kc, ks)


def _na_kernel(q_ref, k_ref, v_ref, gate_ref, bias_ref, o_ref, *, rows, rows_per_step):
    i = pl.program_id(1)
    first = lax.broadcasted_iota(jnp.int32, (GRID_W, LANES), 1) < HEAD_DIM

    def group_body(t2, carry):
        chains = []
        for dt in range(NA_ROWS_PER_GROUP):
            t = t2 * NA_ROWS_PER_GROUP + dt
            r = i * rows_per_step + t
            rs = jnp.clip(r - NA_KH // 2, 0, rows - NA_KH)
            tok = pl.ds(pl.multiple_of(t * GRID_W, GRID_W), GRID_W)
            for hp in range(NA_HEADS // 2):
                chains.append((tok, pl.multiple_of(rs * NA_QBLK, NA_QBLK), slice(hp * LANES, (hp + 1) * LANES),
                               hp, r - rs))
        def window(ref, key0, cs):
            return jnp.concatenate([ref[0, cb, pl.ds(key0, NA_KH * NA_QBLK), cs] for cb in range(NA_CBLKS)], axis=0)

        half = GRID_W // 2
        near = (NA_CBLKS - 1) * LANES
        lanes_of = lambda r0: slice(0, near) if r0 % GRID_W == 0 else slice(LANES, LANES + near)
        scores = []
        for tok, key0, cs, hp, delta in chains:
            q_pair = q_ref[0, tok, cs]
            zero = jnp.zeros_like(q_pair)
            q2 = jnp.concatenate([jnp.where(first, q_pair, zero), jnp.where(first, zero, q_pair)], axis=0)
            s = _dot_nt(q2, window(k_ref, key0, cs))
            s = jnp.concatenate([s[r0:r0 + half, lanes_of(r0)] for r0 in range(0, 2 * GRID_W, half)], axis=0)
            scores.append(s + bias_ref[hp, delta])
        probs = [jnp.exp2(s - jnp.max(s, axis=-1, keepdims=True)) for s in scores]
        sums = [jnp.sum(p, axis=-1, keepdims=True) for p in probs]
        pad = jnp.zeros((half, LANES), BF16)
        outs = []
        for p, l, (tok, key0, cs, hp, delta) in zip(probs, sums, chains):
            p = p.astype(BF16)
            p = jnp.concatenate([jnp.concatenate([p[r0:r0 + half], pad] if r0 % GRID_W == 0 else [pad, p[r0:r0 + half]],
                                                 axis=1) for r0 in range(0, 2 * GRID_W, half)], axis=0)
            outs.append(_dot(p, window(v_ref, key0, cs)) / l)
        for o2, (tok, key0, cs, hp, delta) in zip(outs, chains):
            o = jnp.where(first, o2[:GRID_W], o2[GRID_W:])
            g = gate_ref[0, tok, cs].astype(F32)
            o_ref[0, tok, cs] = (o * jax.nn.silu(g)).astype(BF16)
        return carry

    lax.fori_loop(0, rows_per_step // NA_ROWS_PER_GROUP, group_body, 0)


def _na(q3, k4, v4, gate3, bias, rows_per_step):
    b, s, _ = q3.shape
    rows = s // GRID_W
    tm = rows_per_step * GRID_W
    tile = pl.BlockSpec((1, tm, NA_WIDTH), lambda i, j: (i, j, 0))
    full = pl.BlockSpec((1,) + k4.shape[1:], lambda i, j: (i, 0, 0, 0))
    return pl.pallas_call(
        functools.partial(_na_kernel, rows=rows, rows_per_step=rows_per_step),
        grid=(b, rows // rows_per_step),
        in_specs=[tile, full, full, tile,
                  pl.BlockSpec(bias.shape, lambda i, j: (0, 0, 0, 0))],
        out_specs=tile,
        out_shape=jax.ShapeDtypeStruct((b, s, NA_WIDTH), BF16),
        compiler_params=_params("arbitrary", "arbitrary"),
        name="na_attn",
    )(q3, k4, v4, gate3, bias)


def _na_bias(rpb):
    near = NA_CBLKS - 1
    reach = near * NA_QBLK
    c = np.arange(GRID_W)
    cs = np.clip(c - NA_KW // 2, 0, GRID_W - NA_KW)
    kc = np.arange(reach)[None, :] + np.where(c < GRID_W // 2, 0, NA_QBLK)[:, None]
    valid = (kc >= cs[:, None]) & (kc < cs[:, None] + NA_KW)
    dc = np.clip(kc - c[:, None] + (NA_KW - 1), 0, 2 * NA_KW - 2)
    n_dr, n_dc = 2 * NA_KH - 1, 2 * NA_KW - 1
    onehot = (dc.reshape(-1)[None, :] == np.arange(n_dc)[:, None]).astype(np.float32)
    sel = jnp.dot(rpb.astype(F32).reshape(NA_HEADS * n_dr, n_dc), jnp.asarray(onehot),
                  precision=lax.Precision.HIGHEST)
    sel = sel.reshape(NA_HEADS, n_dr, GRID_W, reach)
    sel = jnp.where(jnp.asarray(valid)[None, None], sel * LOG2E, NEG)
    per_delta = jnp.stack([sel[:, NA_KH - 1 - d:2 * NA_KH - 1 - d] for d in range(NA_KH)], axis=1)
    b = per_delta.reshape(NA_HEADS, NA_KH, NA_KH, GRID_W, near, NA_QBLK)
    b = b.transpose(0, 1, 3, 4, 2, 5)
    width = near * NA_KH * NA_QBLK
    b = b.reshape(NA_HEADS // 2, 2, NA_KH, GRID_W, width)
    return b.transpose(0, 2, 1, 3, 4).reshape(NA_HEADS // 2, NA_KH, 2 * GRID_W, width)


def _merge_kernel(x_ref, af_ref, ana_ref, km_ref, vm_ref, g_ref, wc_ref, wf_ref, wna_ref, wca_ref,
                  wout_ref, gfin_ref, y_ref, *, subtiles):
    tm = x_ref.shape[0] // subtiles
    m0 = 2 * CA_WIDTH
    for st in range(subtiles):
        rows = slice(st * tm, (st + 1) * tm)
        x = x_ref[rows, :]
        h = _rmsnorm(x, g_ref[...]).astype(BF16)
        gate = lambda j: jax.nn.sigmoid(_dot(h, wc_ref[:, m0 + j * D_MODEL:m0 + (j + 1) * D_MODEL]))
        merged = gate(0) * _dot(af_ref[rows, :], wf_ref[...])
        merged = merged + gate(1) * _dot(ana_ref[rows, :], wna_ref[...])
        z_ca = _dot(h, wc_ref[:, :m0])
        q_ca = (z_ca[:, :CA_WIDTH] * (HEAD_DIM ** -0.5)).astype(BF16)
        parts = []
        for hp in range(CA_HEADS // 2):
            cs = slice(hp * LANES, (hp + 1) * LANES)
            o = _pair_attention(q_ca[:, cs], km_ref[0, :, cs], vm_ref[0, :, cs], None)
            parts.append((o * jax.nn.silu(z_ca[:, CA_WIDTH + hp * LANES:CA_WIDTH + (hp + 1) * LANES])).astype(BF16))
        merged = merged + gate(2) * _dot(jnp.concatenate(parts, axis=1), wca_ref[...])
        out = x + _dot(merged.astype(BF16), wout_ref[...])
        y_ref[rows, :] = _rmsnorm(out, gfin_ref[...])


def _merge(x2, a_f, a_na, kv, g_norm, w_c, w_f, w_na, w_ca, w_out, g_final, seq, tm, subtiles):
    n = x2.shape[0]
    per_batch = seq // tm
    const = lambda i: (0, 0)
    once = dict(pipeline_mode=pl.Buffered(1))
    return pl.pallas_call(
        functools.partial(_merge_kernel, subtiles=subtiles),
        grid=(n // tm,),
        in_specs=[pl.BlockSpec((tm, D_MODEL), lambda i: (i, 0)),
                  pl.BlockSpec((tm, F_WIDTH), lambda i: (i, 0)),
                  pl.BlockSpec((tm, NA_WIDTH), lambda i: (i, 0)),
                  pl.BlockSpec((1, N_MEM, CA_WIDTH), lambda i: (i // per_batch, 0, 0)),
                  pl.BlockSpec((1, N_MEM, CA_WIDTH), lambda i: (i // per_batch, 0, 1)),
                  pl.BlockSpec((1, D_MODEL), const),
                  pl.BlockSpec((D_MODEL, C_WIDTH), const, **once),
                  pl.BlockSpec((F_WIDTH, D_MODEL), const, **once),
                  pl.BlockSpec((NA_WIDTH, D_MODEL), const, **once),
                  pl.BlockSpec((CA_WIDTH, D_MODEL), const, **once),
                  pl.BlockSpec((D_MODEL, D_MODEL), const, **once),
                  pl.BlockSpec((1, D_MODEL), const)],
        out_specs=pl.BlockSpec((tm, D_MODEL), lambda i: (i, 0)),
        out_shape=jax.ShapeDtypeStruct((n, D_MODEL), F32),
        compiler_params=_params("arbitrary"),
        name="merge_out",
    )(x2, a_f, a_na, kv, kv, g_norm, w_c, w_f, w_na, w_ca, w_out, g_final)


def _cos_sin(n):
    k = np.arange(n)
    ang = 2.0 * np.pi * ((k[:, None] * k[None, :]) % n) / n
    return np.cos(ang).astype(np.float32), np.sin(ang).astype(np.float32)


def _row_tables(rows):
    seq = rows * GRID_W
    cr, sr = (jnp.asarray(t)[None] for t in _cos_sin(rows))
    k1, col = np.arange(rows)[None, :, None], np.arange(GRID_W)[:, None, None]
    tw = 2.0 * np.pi * ((k1 * col) % seq) / seq
    ct, st = jnp.asarray(np.cos(tw), F32), jnp.asarray(np.sin(tw), F32)
    c, s = cr * ct - sr * st, sr * ct + cr * st
    top = jnp.concatenate([c, -s], axis=2)
    bottom = jnp.concatenate([-s, -c], axis=2)
    return jnp.concatenate([top, bottom], axis=1).astype(BF16)


def _shared_tables():
    c, s = _cos_sin(F_GROUP_DIM)
    eye = np.eye(F_GROUPS, dtype=np.float32)
    cdft = jnp.asarray(np.concatenate([np.kron(eye, c), np.kron(eye, s)], axis=1)).astype(BF16)
    c, s = _cos_sin(GRID_W)
    eye = np.eye(BF16_SUBLANES, dtype=np.float32)
    return cdft, jnp.asarray(np.kron(c, eye)).astype(BF16), jnp.asarray(np.kron(s, eye)).astype(BF16)


def _trunk(x, mem, g_norm, w_a, w_c, bias, g_mem, w_kv, w_f, w_na, w_ca, w_out, g_final, tables):
    cdft, kc, ks = tables
    b, seq, d = x.shape
    rows = seq // GRID_W
    n = b * seq
    x2 = x.reshape(n, d)

    kv = _mem_kv(mem, g_mem, w_kv)
    p, q, gate_f, qn, kn, vn, gate_na = _in_proj(x2, g_norm, w_a, cdft, seq, tm=1024)

    a_f = _fft(p, q, gate_f.reshape(b, GRID_W, rows, F_WIDTH), _row_tables(rows), kc, ks,
               norm=float((seq * F_GROUP_DIM) ** -0.5))

    shp = (b, seq, NA_WIDTH)
    a_na = _na(qn.reshape(shp), kn, vn, gate_na.reshape(shp), bias, rows_per_step=8)

    y = _merge(x2, a_f.reshape(n, F_WIDTH), a_na.reshape(n, NA_WIDTH), kv, g_norm, w_c, w_f, w_na, w_ca,
               w_out, g_final, seq, tm=1024, subtiles=2)
    return y.reshape(b, seq, d)


def kernel(x_prompt, x_sample, mem_prompt, mem_sample, g_norm, w_in, na_rpb, g_mem, w_mem_kv,
           w_f_out, w_na_out, w_ca_out, w_out, g_final):
    assert w_in.shape[0] == 1, "the fused final norm assumes a single layer"
    w_a = w_in[0, :, :A_WIDTH].astype(BF16)
    w_c = w_in[0, :, A_WIDTH:].astype(BF16)
    args = (g_norm[0][None], w_a, w_c, _na_bias(na_rpb[0]), g_mem[0][None], w_mem_kv[0].astype(BF16),
            w_f_out[0].astype(BF16), w_na_out[0].astype(BF16), w_ca_out[0].astype(BF16),
            w_out[0].astype(BF16), g_final[None], _shared_tables())
    return (_trunk(x_prompt, mem_prompt, *args), _trunk(x_sample, mem_sample, *args))
```

```python
import functools

import numpy as np
import jax
import jax.numpy as jnp
from jax import lax
from jax.experimental import pallas as pl
from jax.experimental.pallas import tpu as pltpu

D_MODEL = 1024
GRID_W = 64
N_MEM = 256
F_GROUPS = 4
F_GROUP_DIM = 96
F_WIDTH = F_GROUPS * F_GROUP_DIM
NA_HEADS = 6
HEAD_DIM = 64
NA_WIDTH = NA_HEADS * HEAD_DIM
CA_HEADS = 4
CA_WIDTH = CA_HEADS * HEAD_DIM
NA_KH = 8
NA_KW = 16
NA_QBLK = 16
NA_CBLKS = GRID_W // NA_QBLK
EPS = 1e-6
NEG = -1e30
LOG2E = float(np.log2(np.e))

LANES = 128
BF16_SUBLANES = 16
NA_ROWS_PER_GROUP = 16
ROW_PITCH = 72
A_WIDTH = 2 * F_WIDTH + 4 * NA_WIDTH
C_WIDTH = 2 * CA_WIDTH + 3 * D_MODEL
VMEM_LIMIT = 56 * 1024 * 1024

BF16 = jnp.bfloat16
F32 = jnp.float32


def _rmsnorm(x, g):
    return x * lax.rsqrt(jnp.mean(x * x, axis=-1, keepdims=True) + EPS) * g


def _dot(a, b):
    return jnp.dot(a, b, preferred_element_type=F32)


def _dot_nt(a, b):
    return lax.dot_general(a, b, (((1,), (1,)), ((), ())), preferred_element_type=F32)


def _params(*semantics):
    return pltpu.CompilerParams(dimension_semantics=semantics, vmem_limit_bytes=VMEM_LIMIT)


def _pair_attention(q_pair, k_pair, v_pair, bias):
    m = q_pair.shape[0]
    first = lax.broadcasted_iota(jnp.int32, (m, LANES), 1) < HEAD_DIM
    zero = jnp.zeros_like(q_pair)
    q2 = jnp.concatenate([jnp.where(first, q_pair, zero), jnp.where(first, zero, q_pair)], axis=0)
    s = _dot_nt(q2, k_pair)
    if bias is not None:
        s = s + bias
    p = jnp.exp(s - jnp.max(s, axis=-1, keepdims=True))
    l = jnp.sum(p, axis=-1, keepdims=True)
    o2 = _dot(p.astype(BF16), v_pair) / l
    return jnp.where(first, o2[:m], o2[m:])


def _mem_kv_kernel(mem_ref, g_ref, w_ref, kv_ref):
    nb = mem_ref.shape[0]
    h = _rmsnorm(mem_ref[...].reshape(nb * N_MEM, D_MODEL), g_ref[...]).astype(BF16)
    kv_ref[...] = _dot(h, w_ref[...]).astype(BF16).reshape(nb, N_MEM, 2 * CA_WIDTH)


def _mem_kv(mem, g_mem, w_kv):
    b = mem.shape[0]
    nb = int(np.gcd(b, 4))
    return pl.pallas_call(
        _mem_kv_kernel,
        grid=(b // nb,),
        in_specs=[pl.BlockSpec((nb, N_MEM, D_MODEL), lambda i: (i, 0, 0)),
                  pl.BlockSpec((1, D_MODEL), lambda i: (0, 0)),
                  pl.BlockSpec((D_MODEL, 2 * CA_WIDTH), lambda i: (0, 0))],
        out_specs=pl.BlockSpec((nb, N_MEM, 2 * CA_WIDTH), lambda i: (i, 0, 0)),
        out_shape=jax.ShapeDtypeStruct((b, N_MEM, 2 * CA_WIDTH), BF16),
        compiler_params=_params("arbitrary"),
        name="mem_kv",
    )(mem, g_mem, w_kv)


def _store_column_slabs(ref, tokens, grid_rows):
    for r in range(grid_rows):
        for cb in range(NA_CBLKS):
            t0 = r * GRID_W + cb * NA_QBLK
            ref[0, cb, r * NA_QBLK:(r + 1) * NA_QBLK, :] = tokens[t0:t0 + NA_QBLK]


def _in_proj_kernel(x_ref, g_ref, w_ref, cdft_ref, p_ref, q_ref, gf_ref, qn_ref, kn_ref, vn_ref, gn_ref,
                    t_ref, *, grid_rows):
    w = F_WIDTH
    slabs = 2 * w // LANES
    x = x_ref[...]
    rs = lax.rsqrt(jnp.mean(x * x, axis=-1, keepdims=True) + EPS)
    xb = (x * g_ref[...]).astype(BF16)
    zcols = lambda j: _dot(xb, w_ref[:, 2 * j * w:2 * (j + 1) * w])
    z_f = zcols(0)
    z_qk = zcols(1)
    pq = _dot(z_f[:, :w].astype(BF16), cdft_ref[...]) * rs
    for r in range(grid_rows):
        for s in range(slabs):
            t_ref[s, r * ROW_PITCH:r * ROW_PITCH + GRID_W, :] = pq[r * GRID_W:(r + 1) * GRID_W,
                                                                   s * LANES:(s + 1) * LANES]
    gf_ref[...] = (z_f[:, w:] * rs).astype(BF16)
    qn_ref[...] = (z_qk[:, :w] * (rs * (HEAD_DIM ** -0.5 * LOG2E))).astype(BF16)
    _store_column_slabs(kn_ref, (z_qk[:, w:] * rs).astype(BF16), grid_rows)
    z_vg = zcols(2)
    _store_column_slabs(vn_ref, (z_vg[:, :w] * rs).astype(BF16), grid_rows)
    gn_ref[...] = (z_vg[:, w:] * rs).astype(BF16)
    for c in range(GRID_W):
        for s in range(slabs):
            v = t_ref[s, pl.ds(c, grid_rows, stride=ROW_PITCH), :].astype(BF16)
            dst = p_ref if s < slabs // 2 else q_ref
            dst[0, s % (slabs // 2), :, c * LANES:(c + 1) * LANES] = v


def _in_proj(x2, g_norm, w_a, cdft, seq, tm):
    n = x2.shape[0]
    grid_rows = tm // GRID_W
    per_batch = seq // tm
    out = jax.ShapeDtypeStruct((n, F_WIDTH), BF16)
    ospec = pl.BlockSpec((tm, F_WIDTH), lambda i: (i, 0))
    blocks = F_WIDTH // LANES
    wide = jax.ShapeDtypeStruct((n // seq, blocks, seq // GRID_W, GRID_W * LANES), BF16)
    wspec = pl.BlockSpec((1, blocks, grid_rows, GRID_W * LANES), lambda i: (i // per_batch, 0, i % per_batch, 0))
    slabs = jax.ShapeDtypeStruct((n // seq, NA_CBLKS, seq // NA_CBLKS, NA_WIDTH), BF16)
    sspec = pl.BlockSpec((1, NA_CBLKS, grid_rows * NA_QBLK, NA_WIDTH), lambda i: (i // per_batch, 0, i % per_batch, 0))
    return pl.pallas_call(
        functools.partial(_in_proj_kernel, grid_rows=grid_rows),
        grid=(n // tm,),
        in_specs=[pl.BlockSpec((tm, D_MODEL), lambda i: (i, 0)),
                  pl.BlockSpec((1, D_MODEL), lambda i: (0, 0)),
                  pl.BlockSpec((D_MODEL, A_WIDTH), lambda i: (0, 0)),
                  pl.BlockSpec((F_WIDTH, 2 * F_WIDTH), lambda i: (0, 0))],
        out_specs=[wspec, wspec, ospec, ospec, sspec, sspec, ospec],
        out_shape=[wide, wide, out, out, slabs, slabs, out],
        scratch_shapes=[pltpu.VMEM((2 * F_WIDTH // LANES, grid_rows * ROW_PITCH, LANES), F32)],
        compiler_params=_params("arbitrary"),
        name="in_proj",
    )(x2, g_norm, w_a, cdft)


def _fft_kernel(p_ref, q_ref, gate_ref, w_ref, kc_ref, ks_ref, o_ref, ar_ref, ai_ref, *, rows, norm):
    for j in range(GRID_W):
        sl = slice(j * LANES, (j + 1) * LANES)
        pq = jnp.concatenate([p_ref[0, 0, :, sl], q_ref[0, 0, :, sl]], axis=0)
        r = _dot(w_ref[j], pq)
        ar_ref[j] = r[:rows].astype(BF16)
        ai_ref[j] = r[rows:].astype(BF16)
    m = GRID_W * BF16_SUBLANES
    tile = lambda ref, t: ref[:, t * BF16_SUBLANES:(t + 1) * BF16_SUBLANES, :].reshape(m, LANES)
    for tp in range(rows // (2 * BF16_SUBLANES)):
        xr = jnp.concatenate([tile(ar_ref, 2 * tp), tile(ar_ref, 2 * tp + 1)], axis=1)
        xi = jnp.concatenate([tile(ai_ref, 2 * tp), tile(ai_ref, 2 * tp + 1)], axis=1)
        y = _dot(kc_ref[...], xr) + _dot(ks_ref[...], xi)
        for half in range(2):
            k1 = slice((2 * tp + half) * BF16_SUBLANES, (2 * tp + half + 1) * BF16_SUBLANES)
            g = gate_ref[0, :, k1, :].reshape(m, LANES).astype(F32)
            out = (y[:, half * LANES:(half + 1) * LANES] * norm * jax.nn.silu(g)).astype(BF16)
            o_ref[0, :, k1, :] = out.reshape(GRID_W, BF16_SUBLANES, LANES)


def _fft(p4, q4, gate4, w, kc, ks, norm):
    b, blocks, rows, _ = p4.shape
    assert rows % (2 * BF16_SUBLANES) == 0, "the column DFT handles k1 in pairs of packed tiles"
    m = GRID_W * BF16_SUBLANES
    once = dict(pipeline_mode=pl.Buffered(1))
    in_spec = pl.BlockSpec((1, 1, rows, GRID_W * LANES), lambda i, c: (i, c, 0, 0))
    tok_spec = pl.BlockSpec((1, GRID_W, rows, LANES), lambda i, c: (i, 0, 0, c))
    return pl.pallas_call(
        functools.partial(_fft_kernel, rows=rows, norm=norm),
        grid=(b, blocks),
        in_specs=[in_spec, in_spec, tok_spec,
                  pl.BlockSpec((GRID_W, 2 * rows, 2 * rows), lambda i, c: (0, 0, 0), **once),
                  pl.BlockSpec((m, m), lambda i, c: (0, 0), **once),
                  pl.BlockSpec((m, m), lambda i, c: (0, 0), **once)],
        out_specs=tok_spec,
        out_shape=jax.ShapeDtypeStruct((b, GRID_W, rows, F_WIDTH), BF16),
        scratch_shapes=[pltpu.VMEM((GRID_W, rows, LANES), BF16), pltpu.VMEM((GRID_W, rows, LANES), BF16)],
        compiler_params=_params("arbitrary", "arbitrary"),
        name="fft",
    )(p4, q4, gate4, w, kc, ks)


def _na_kernel(q_ref, k_ref, v_ref, gate_ref, bias_ref, o_ref, *, rows, rows_per_step):
    i = pl.program_id(1)
    first = lax.broadcasted_iota(jnp.int32, (GRID_W, LANES), 1) < HEAD_DIM

    def group_body(t2, carry):
        chains = []
        for dt in range(NA_ROWS_PER_GROUP):
            t = t2 * NA_ROWS_PER_GROUP + dt
            r = i * rows_per_step + t
            rs = jnp.clip(r - NA_KH // 2, 0, rows - NA_KH)
            tok = pl.ds(pl.multiple_of(t * GRID_W, GRID_W), GRID_W)
            for hp in range(NA_HEADS // 2):
                chains.append((tok, pl.multiple_of(rs * NA_QBLK, NA_QBLK), slice(hp * LANES, (hp + 1) * LANES),
                               hp, r - rs))
        def window(ref, key0, cs):
            return jnp.concatenate([ref[0, cb, pl.ds(key0, NA_KH * NA_QBLK), cs] for cb in range(NA_CBLKS)], axis=0)

        half = GRID_W // 2
        near = (NA_CBLKS - 1) * LANES
        lanes_of = lambda r0: slice(0, near) if r0 % GRID_W == 0 else slice(LANES, LANES + near)
        scores = []
        for tok, key0, cs, hp, delta in chains:
            q_pair = q_ref[0, tok, cs]
            zero = jnp.zeros_like(q_pair)
            q2 = jnp.concatenate([jnp.where(first, q_pair, zero), jnp.where(first, zero, q_pair)], axis=0)
            s = _dot_nt(q2, window(k_ref, key0, cs))
            s = jnp.concatenate([s[r0:r0 + half, lanes_of(r0)] for r0 in range(0, 2 * GRID_W, half)], axis=0)
            scores.append(s + bias_ref[hp, delta])
        probs = [jnp.exp2(s - jnp.max(s, axis=-1, keepdims=True)) for s in scores]
        sums = [jnp.sum(p, axis=-1, keepdims=True) for p in probs]
        pad = jnp.zeros((half, LANES), BF16)
        outs = []
        for p, l, (tok, key0, cs, hp, delta) in zip(probs, sums, chains):
            p = p.astype(BF16)
            p = jnp.concatenate([jnp.concatenate([p[r0:r0 + half], pad] if r0 % GRID_W == 0 else [pad, p[r0:r0 + half]],
                                                 axis=1) for r0 in range(0, 2 * GRID_W, half)], axis=0)
            outs.append(_dot(p, window(v_ref, key0, cs)) / l)
        for o2, (tok, key0, cs, hp, delta) in zip(outs, chains):
            o = jnp.where(first, o2[:GRID_W], o2[GRID_W:])
            g = gate_ref[0, tok, cs].astype(F32)
            o_ref[0, tok, cs] = (o * jax.nn.silu(g)).astype(BF16)
        return carry

    lax.fori_loop(0, rows_per_step // NA_ROWS_PER_GROUP, group_body, 0)


def _na(q3, k4, v4, gate3, bias, rows_per_step):
    b, s, _ = q3.shape
    rows = s // GRID_W
    tm = rows_per_step * GRID_W
    tile = pl.BlockSpec((1, tm, NA_WIDTH), lambda i, j: (i, j, 0))
    full = pl.BlockSpec((1,) + k4.shape[1:], lambda i, j: (i, 0, 0, 0))
    return pl.pallas_call(
        functools.partial(_na_kernel, rows=rows, rows_per_step=rows_per_step),
        grid=(b, rows // rows_per_step),
        in_specs=[tile, full, full, tile,
                  pl.BlockSpec(bias.shape, lambda i, j: (0, 0, 0, 0))],
        out_specs=tile,
        out_shape=jax.ShapeDtypeStruct((b, s, NA_WIDTH), BF16),
        compiler_params=_params("arbitrary", "arbitrary"),
        name="na_attn",
    )(q3, k4, v4, gate3, bias)


def _na_bias(rpb):
    near = NA_CBLKS - 1
    reach = near * NA_QBLK
    c = np.arange(GRID_W)
    cs = np.clip(c - NA_KW // 2, 0, GRID_W - NA_KW)
    kc = np.arange(reach)[None, :] + np.where(c < GRID_W // 2, 0, NA_QBLK)[:, None]
    valid = (kc >= cs[:, None]) & (kc < cs[:, None] + NA_KW)
    dc = np.clip(kc - c[:, None] + (NA_KW - 1), 0, 2 * NA_KW - 2)
    n_dr, n_dc = 2 * NA_KH - 1, 2 * NA_KW - 1
    onehot = (dc.reshape(-1)[None, :] == np.arange(n_dc)[:, None]).astype(np.float32)
    sel = jnp.dot(rpb.astype(F32).reshape(NA_HEADS * n_dr, n_dc), jnp.asarray(onehot),
                  precision=lax.Precision.HIGHEST)
    sel = sel.reshape(NA_HEADS, n_dr, GRID_W, reach)
    sel = jnp.where(jnp.asarray(valid)[None, None], sel * LOG2E, NEG)
    per_delta = jnp.stack([sel[:, NA_KH - 1 - d:2 * NA_KH - 1 - d] for d in range(NA_KH)], axis=1)
    b = per_delta.reshape(NA_HEADS, NA_KH, NA_KH, GRID_W, near, NA_QBLK)
    b = b.transpose(0, 1, 3, 4, 2, 5)
    width = near * NA_KH * NA_QBLK
    b = b.reshape(NA_HEADS // 2, 2, NA_KH, GRID_W, width)
    return b.transpose(0, 2, 1, 3, 4).reshape(NA_HEADS // 2, NA_KH, 2 * GRID_W, width)


def _merge_kernel(x_ref, af_ref, ana_ref, km_ref, vm_ref, g_ref, wc_ref, wf_ref, wna_ref, wca_ref,
                  wout_ref, gfin_ref, y_ref, *, subtiles):
    tm = x_ref.shape[0] // subtiles
    m0 = 2 * CA_WIDTH
    for st in range(subtiles):
        rows = slice(st * tm, (st + 1) * tm)
        x = x_ref[rows, :]
        h = _rmsnorm(x, g_ref[...]).astype(BF16)
        gate = lambda j: jax.nn.sigmoid(_dot(h, wc_ref[:, m0 + j * D_MODEL:m0 + (j + 1) * D_MODEL]))
        merged = gate(0) * _dot(af_ref[rows, :], wf_ref[...])
        merged = merged + gate(1) * _dot(ana_ref[rows, :], wna_ref[...])
        z_ca = _dot(h, wc_ref[:, :m0])
        q_ca = (z_ca[:, :CA_WIDTH] * (HEAD_DIM ** -0.5)).astype(BF16)
        parts = []
        for hp in range(CA_HEADS // 2):
            cs = slice(hp * LANES, (hp + 1) * LANES)
            o = _pair_attention(q_ca[:, cs], km_ref[0, :, cs], vm_ref[0, :, cs], None)
            parts.append((o * jax.nn.silu(z_ca[:, CA_WIDTH + hp * LANES:CA_WIDTH + (hp + 1) * LANES])).astype(BF16))
        merged = merged + gate(2) * _dot(jnp.concatenate(parts, axis=1), wca_ref[...])
        out = x + _dot(merged.astype(BF16), wout_ref[...])
        y_ref[rows, :] = _rmsnorm(out, gfin_ref[...])


def _merge(x2, a_f, a_na, kv, g_norm, w_c, w_f, w_na, w_ca, w_out, g_final, seq, tm, subtiles):
    n = x2.shape[0]
    per_batch = seq // tm
    const = lambda i: (0, 0)
    once = dict(pipeline_mode=pl.Buffered(1))
    return pl.pallas_call(
        functools.partial(_merge_kernel, subtiles=subtiles),
        grid=(n // tm,),
        in_specs=[pl.BlockSpec((tm, D_MODEL), lambda i: (i, 0)),
                  pl.BlockSpec((tm, F_WIDTH), lambda i: (i, 0)),
                  pl.BlockSpec((tm, NA_WIDTH), lambda i: (i, 0)),
                  pl.BlockSpec((1, N_MEM, CA_WIDTH), lambda i: (i // per_batch, 0, 0)),
                  pl.BlockSpec((1, N_MEM, CA_WIDTH), lambda i: (i // per_batch, 0, 1)),
                  pl.BlockSpec((1, D_MODEL), const),
                  pl.BlockSpec((D_MODEL, C_WIDTH), const, **once),
                  pl.BlockSpec((F_WIDTH, D_MODEL), const, **once),
                  pl.BlockSpec((NA_WIDTH, D_MODEL), const, **once),
                  pl.BlockSpec((CA_WIDTH, D_MODEL), const, **once),
                  pl.BlockSpec((D_MODEL, D_MODEL), const, **once),
                  pl.BlockSpec((1, D_MODEL), const)],
        out_specs=pl.BlockSpec((tm, D_MODEL), lambda i: (i, 0)),
        out_shape=jax.ShapeDtypeStruct((n, D_MODEL), F32),
        compiler_params=_params("arbitrary"),
        name="merge_out",
    )(x2, a_f, a_na, kv, kv, g_norm, w_c, w_f, w_na, w_ca, w_out, g_final)


def _cos_sin(n):
    k = np.arange(n)
    ang = 2.0 * np.pi * ((k[:, None] * k[None, :]) % n) / n
    return np.cos(ang).astype(np.float32), np.sin(ang).astype(np.float32)


def _row_tables(rows):
    seq = rows * GRID_W
    cr, sr = (jnp.asarray(t)[None] for t in _cos_sin(rows))
    k1, col = np.arange(rows)[None, :, None], np.arange(GRID_W)[:, None, None]
    tw = 2.0 * np.pi * ((k1 * col) % seq) / seq
    ct, st = jnp.asarray(np.cos(tw), F32), jnp.asarray(np.sin(tw), F32)
    c, s = cr * ct - sr * st, sr * ct + cr * st
    top = jnp.concatenate([c, -s], axis=2)
    bottom = jnp.concatenate([-s, -c], axis=2)
    return jnp.concatenate([top, bottom], axis=1).astype(BF16)


def _shared_tables():
    c, s = _cos_sin(F_GROUP_DIM)
    eye = np.eye(F_GROUPS, dtype=np.float32)
    cdft = jnp.asarray(np.concatenate([np.kron(eye, c), np.kron(eye, s)], axis=1)).astype(BF16)
    c, s = _cos_sin(GRID_W)
    eye = np.eye(BF16_SUBLANES, dtype=np.float32)
    return cdft, jnp.asarray(np.kron(c, eye)).astype(BF16), jnp.asarray(np.kron(s, eye)).astype(BF16)


def _trunk(x, mem, g_norm, w_a, w_c, bias, g_mem, w_kv, w_f, w_na, w_ca, w_out, g_final, tables):
    cdft, kc, ks = tables
    b, seq, d = x.shape
    rows = seq // GRID_W
    n = b * seq
    x2 = x.reshape(n, d)

    kv = _mem_kv(mem, g_mem, w_kv)
    p, q, gate_f, qn, kn, vn, gate_na = _in_proj(x2, g_norm, w_a, cdft, seq, tm=1024)

    a_f = _fft(p, q, gate_f.reshape(b, GRID_W, rows, F_WIDTH), _row_tables(rows), kc, ks,
               norm=float((seq * F_GROUP_DIM) ** -0.5))

    shp = (b, seq, NA_WIDTH)
    a_na = _na(qn.reshape(shp), kn, vn, gate_na.reshape(shp), bias, rows_per_step=16)

    y = _merge(x2, a_f.reshape(n, F_WIDTH), a_na.reshape(n, NA_WIDTH), kv, g_norm, w_c, w_f, w_na, w_ca,
               w_out, g_final, seq, tm=1024, subtiles=2)
    return y.reshape(b, seq, d)


def kernel(x_prompt, x_sample, mem_prompt, mem_sample, g_norm, w_in, na_rpb, g_mem, w_mem_kv,
           w_f_out, w_na_out, w_ca_out, w_out, g_final):
    assert w_in.shape[0] == 1, "the fused final norm assumes a single layer"
    w_a = w_in[0, :, :A_WIDTH].astype(BF16)
    w_c = w_in[0, :, A_WIDTH:].astype(BF16)
    args = (g_norm[0][None], w_a, w_c, _na_bias(na_rpb[0]), g_mem[0][None], w_mem_kv[0].astype(BF16),
            w_f_out[0].astype(BF16), w_na_out[0].astype(BF16), w_ca_out[0].astype(BF16),
            w_out[0].astype(BF16), g_final[None], _shared_tables())
    return (_trunk(x_prompt, mem_prompt, *args), _trunk(x_sample, mem_sample, *args))
```

```python
import functools

import numpy as np
import jax
import jax.numpy as jnp
from jax import lax
from jax.experimental import pallas as pl
from jax.experimental.pallas import tpu as pltpu

D_MODEL = 1024
GRID_W = 64
N_MEM = 256
F_GROUPS = 4
F_GROUP_DIM = 96
F_WIDTH = F_GROUPS * F_GROUP_DIM
NA_HEADS = 6
HEAD_DIM = 64
NA_WIDTH = NA_HEADS * HEAD_DIM
CA_HEADS = 4
CA_WIDTH = CA_HEADS * HEAD_DIM
NA_KH = 8
NA_KW = 16
NA_QBLK = 16
NA_CBLKS = GRID_W // NA_QBLK
EPS = 1e-6
NEG = -1e30
LOG2E = float(np.log2(np.e))

LANES = 128
BF16_SUBLANES = 16
ROW_PITCH = 72
A_WIDTH = 2 * F_WIDTH + 4 * NA_WIDTH
C_WIDTH = 2 * CA_WIDTH + 3 * D_MODEL
VMEM_LIMIT = 56 * 1024 * 1024

IN_PROJ_TOKENS = 1024
MERGE_TOKENS = 1024
MERGE_SUBTILES = 2
NA_ROWS_PER_STEP = 32
NA_ROWS_PER_GROUP = 16

BF16 = jnp.bfloat16
F32 = jnp.float32


def _rmsnorm(x, g):
    return x * lax.rsqrt(jnp.mean(x * x, axis=-1, keepdims=True) + EPS) * g


def _dot(a, b):
    return jnp.dot(a, b, preferred_element_type=F32)


def _dot_nt(a, b):
    return lax.dot_general(a, b, (((1,), (1,)), ((), ())), preferred_element_type=F32)


def _params(*semantics):
    return pltpu.CompilerParams(dimension_semantics=semantics, vmem_limit_bytes=VMEM_LIMIT)


def _pair_attention(q_pair, k_pair, v_pair, bias):
    m = q_pair.shape[0]
    first = lax.broadcasted_iota(jnp.int32, (m, LANES), 1) < HEAD_DIM
    zero = jnp.zeros_like(q_pair)
    q2 = jnp.concatenate([jnp.where(first, q_pair, zero), jnp.where(first, zero, q_pair)], axis=0)
    s = _dot_nt(q2, k_pair)
    if bias is not None:
        s = s + bias
    p = jnp.exp(s - jnp.max(s, axis=-1, keepdims=True))
    l = jnp.sum(p, axis=-1, keepdims=True)
    o2 = _dot(p.astype(BF16), v_pair) / l
    return jnp.where(first, o2[:m], o2[m:])


def _mem_kv_kernel(mem_ref, g_ref, w_ref, kv_ref):
    nb = mem_ref.shape[0]
    h = _rmsnorm(mem_ref[...].reshape(nb * N_MEM, D_MODEL), g_ref[...]).astype(BF16)
    kv_ref[...] = _dot(h, w_ref[...]).astype(BF16).reshape(nb, N_MEM, 2 * CA_WIDTH)


def _mem_kv(mem, g_mem, w_kv):
    b = mem.shape[0]
    nb = int(np.gcd(b, 4))
    return pl.pallas_call(
        _mem_kv_kernel,
        grid=(b // nb,),
        in_specs=[pl.BlockSpec((nb, N_MEM, D_MODEL), lambda i: (i, 0, 0)),
                  pl.BlockSpec((1, D_MODEL), lambda i: (0, 0)),
                  pl.BlockSpec((D_MODEL, 2 * CA_WIDTH), lambda i: (0, 0))],
        out_specs=pl.BlockSpec((nb, N_MEM, 2 * CA_WIDTH), lambda i: (i, 0, 0)),
        out_shape=jax.ShapeDtypeStruct((b, N_MEM, 2 * CA_WIDTH), BF16),
        compiler_params=_params("arbitrary"),
        name="mem_kv",
    )(mem, g_mem, w_kv)


def _store_column_slabs(ref, tokens, grid_rows):
    for r in range(grid_rows):
        for cb in range(NA_CBLKS):
            t0 = r * GRID_W + cb * NA_QBLK
            ref[0, cb, r * NA_QBLK:(r + 1) * NA_QBLK, :] = tokens[t0:t0 + NA_QBLK]


def _in_proj_kernel(x_ref, g_ref, w_ref, cdft_ref, p_ref, q_ref, gf_ref, qn_ref, kn_ref, vn_ref, gn_ref,
                    t_ref, *, grid_rows):
    w = F_WIDTH
    slabs = 2 * w // LANES
    x = x_ref[...]
    rs = lax.rsqrt(jnp.mean(x * x, axis=-1, keepdims=True) + EPS)
    xb = (x * g_ref[...]).astype(BF16)
    zcols = lambda j: _dot(xb, w_ref[:, 2 * j * w:2 * (j + 1) * w])
    z_f = zcols(0)
    z_qk = zcols(1)
    pq = _dot(z_f[:, :w].astype(BF16), cdft_ref[...]) * rs
    for r in range(grid_rows):
        for s in range(slabs):
            t_ref[s, r * ROW_PITCH:r * ROW_PITCH + GRID_W, :] = pq[r * GRID_W:(r + 1) * GRID_W,
                                                                   s * LANES:(s + 1) * LANES]
    gf_ref[...] = (z_f[:, w:] * rs).astype(BF16)
    qn_ref[...] = (z_qk[:, :w] * (rs * (HEAD_DIM ** -0.5 * LOG2E))).astype(BF16)
    _store_column_slabs(kn_ref, (z_qk[:, w:] * rs).astype(BF16), grid_rows)
    z_vg = zcols(2)
    _store_column_slabs(vn_ref, (z_vg[:, :w] * rs).astype(BF16), grid_rows)
    gn_ref[...] = (z_vg[:, w:] * rs).astype(BF16)
    for c in range(GRID_W):
        for s in range(slabs):
            v = t_ref[s, pl.ds(c, grid_rows, stride=ROW_PITCH), :].astype(BF16)
            dst = p_ref if s < slabs // 2 else q_ref
            dst[0, s % (slabs // 2), :, c * LANES:(c + 1) * LANES] = v


def _in_proj(x2, g_norm, w_a, cdft, seq, tm):
    n = x2.shape[0]
    grid_rows = tm // GRID_W
    per_batch = seq // tm
    out = jax.ShapeDtypeStruct((n, F_WIDTH), BF16)
    ospec = pl.BlockSpec((tm, F_WIDTH), lambda i: (i, 0))
    blocks = F_WIDTH // LANES
    wide = jax.ShapeDtypeStruct((n // seq, blocks, seq // GRID_W, GRID_W * LANES), BF16)
    wspec = pl.BlockSpec((1, blocks, grid_rows, GRID_W * LANES), lambda i: (i // per_batch, 0, i % per_batch, 0))
    slabs = jax.ShapeDtypeStruct((n // seq, NA_CBLKS, seq // NA_CBLKS, NA_WIDTH), BF16)
    sspec = pl.BlockSpec((1, NA_CBLKS, grid_rows * NA_QBLK, NA_WIDTH), lambda i: (i // per_batch, 0, i % per_batch, 0))
    return pl.pallas_call(
        functools.partial(_in_proj_kernel, grid_rows=grid_rows),
        grid=(n // tm,),
        in_specs=[pl.BlockSpec((tm, D_MODEL), lambda i: (i, 0)),
                  pl.BlockSpec((1, D_MODEL), lambda i: (0, 0)),
                  pl.BlockSpec((D_MODEL, A_WIDTH), lambda i: (0, 0)),
                  pl.BlockSpec((F_WIDTH, 2 * F_WIDTH), lambda i: (0, 0))],
        out_specs=[wspec, wspec, ospec, ospec, sspec, sspec, ospec],
        out_shape=[wide, wide, out, out, slabs, slabs, out],
        scratch_shapes=[pltpu.VMEM((2 * F_WIDTH // LANES, grid_rows * ROW_PITCH, LANES), F32)],
        compiler_params=_params("arbitrary"),
        name="in_proj",
    )(x2, g_norm, w_a, cdft)


def _fft_kernel(p_ref, q_ref, gate_ref, w_ref, kc_ref, ks_ref, o_ref, ar_ref, ai_ref, *, rows, norm):
    for j in range(GRID_W):
        sl = slice(j * LANES, (j + 1) * LANES)
        pq = jnp.concatenate([p_ref[0, 0, :, sl], q_ref[0, 0, :, sl]], axis=0)
        r = _dot(w_ref[j], pq)
        ar_ref[j] = r[:rows].astype(BF16)
        ai_ref[j] = r[rows:].astype(BF16)
    m = GRID_W * BF16_SUBLANES
    tile = lambda ref, t: ref[:, t * BF16_SUBLANES:(t + 1) * BF16_SUBLANES, :].reshape(m, LANES)
    for tp in range(rows // (2 * BF16_SUBLANES)):
        xr = jnp.concatenate([tile(ar_ref, 2 * tp), tile(ar_ref, 2 * tp + 1)], axis=1)
        xi = jnp.concatenate([tile(ai_ref, 2 * tp), tile(ai_ref, 2 * tp + 1)], axis=1)
        y = _dot(kc_ref[...], xr) + _dot(ks_ref[...], xi)
        for half in range(2):
            k1 = slice((2 * tp + half) * BF16_SUBLANES, (2 * tp + half + 1) * BF16_SUBLANES)
            g = gate_ref[0, :, k1, :].reshape(m, LANES).astype(F32)
            out = (y[:, half * LANES:(half + 1) * LANES] * norm * jax.nn.silu(g)).astype(BF16)
            o_ref[0, :, k1, :] = out.reshape(GRID_W, BF16_SUBLANES, LANES)


def _fft(p4, q4, gate4, w, kc, ks, norm):
    b, blocks, rows, _ = p4.shape
    assert rows % (2 * BF16_SUBLANES) == 0, "the column DFT handles k1 in pairs of packed tiles"
    m = GRID_W * BF16_SUBLANES
    once = dict(pipeline_mode=pl.Buffered(1))
    in_spec = pl.BlockSpec((1, 1, rows, GRID_W * LANES), lambda i, c: (i, c, 0, 0))
    tok_spec = pl.BlockSpec((1, GRID_W, rows, LANES), lambda i, c: (i, 0, 0, c))
    return pl.pallas_call(
        functools.partial(_fft_kernel, rows=rows, norm=norm),
        grid=(b, blocks),
        in_specs=[in_spec, in_spec, tok_spec,
                  pl.BlockSpec((GRID_W, 2 * rows, 2 * rows), lambda i, c: (0, 0, 0), **once),
                  pl.BlockSpec((m, m), lambda i, c: (0, 0), **once),
                  pl.BlockSpec((m, m), lambda i, c: (0, 0), **once)],
        out_specs=tok_spec,
        out_shape=jax.ShapeDtypeStruct((b, GRID_W, rows, F_WIDTH), BF16),
        scratch_shapes=[pltpu.VMEM((GRID_W, rows, LANES), BF16), pltpu.VMEM((GRID_W, rows, LANES), BF16)],
        compiler_params=_params("arbitrary", "arbitrary"),
        name="fft",
    )(p4, q4, gate4, w, kc, ks)


def _na_kernel(q_ref, k_ref, v_ref, gate_ref, bias_ref, o_ref, *, rows, rows_per_step):
    i = pl.program_id(1)
    first = lax.broadcasted_iota(jnp.int32, (GRID_W, LANES), 1) < HEAD_DIM

    def group_body(t2, carry):
        chains = []
        for dt in range(NA_ROWS_PER_GROUP):
            t = t2 * NA_ROWS_PER_GROUP + dt
            r = i * rows_per_step + t
            rs = jnp.clip(r - NA_KH // 2, 0, rows - NA_KH)
            tok = pl.ds(pl.multiple_of(t * GRID_W, GRID_W), GRID_W)
            for hp in range(NA_HEADS // 2):
                chains.append((tok, pl.multiple_of(rs * NA_QBLK, NA_QBLK), slice(hp * LANES, (hp + 1) * LANES),
                               hp, r - rs))
        def window(ref, key0, cs):
            return jnp.concatenate([ref[0, cb, pl.ds(key0, NA_KH * NA_QBLK), cs] for cb in range(NA_CBLKS)], axis=0)

        half = GRID_W // 2
        near = (NA_CBLKS - 1) * LANES
        lanes_of = lambda r0: slice(0, near) if r0 % GRID_W == 0 else slice(LANES, LANES + near)
        scores = []
        for tok, key0, cs, hp, delta in chains:
            q_pair = q_ref[0, tok, cs]
            zero = jnp.zeros_like(q_pair)
            q2 = jnp.concatenate([jnp.where(first, q_pair, zero), jnp.where(first, zero, q_pair)], axis=0)
            s = _dot_nt(q2, window(k_ref, key0, cs))
            s = jnp.concatenate([s[r0:r0 + half, lanes_of(r0)] for r0 in range(0, 2 * GRID_W, half)], axis=0)
            scores.append(s + bias_ref[hp, delta])
        probs = [jnp.exp2(s - jnp.max(s, axis=-1, keepdims=True)) for s in scores]
        sums = [jnp.sum(p, axis=-1, keepdims=True) for p in probs]
        pad = jnp.zeros((half, LANES), BF16)
        outs = []
        for p, l, (tok, key0, cs, hp, delta) in zip(probs, sums, chains):
            p = p.astype(BF16)
            p = jnp.concatenate([jnp.concatenate([p[r0:r0 + half], pad] if r0 % GRID_W == 0 else [pad, p[r0:r0 + half]],
                                                 axis=1) for r0 in range(0, 2 * GRID_W, half)], axis=0)
            outs.append(_dot(p, window(v_ref, key0, cs)) / l)
        for o2, (tok, key0, cs, hp, delta) in zip(outs, chains):
            o = jnp.where(first, o2[:GRID_W], o2[GRID_W:])
            g = gate_ref[0, tok, cs].astype(F32)
            o_ref[0, tok, cs] = (o * jax.nn.silu(g)).astype(BF16)
        return carry

    lax.fori_loop(0, rows_per_step // NA_ROWS_PER_GROUP, group_body, 0)


def _na(q3, k4, v4, gate3, bias, rows_per_step):
    b, s, _ = q3.shape
    rows = s // GRID_W
    tm = rows_per_step * GRID_W
    tile = pl.BlockSpec((1, tm, NA_WIDTH), lambda i, j: (i, j, 0))
    full = pl.BlockSpec((1,) + k4.shape[1:], lambda i, j: (i, 0, 0, 0))
    return pl.pallas_call(
        functools.partial(_na_kernel, rows=rows, rows_per_step=rows_per_step),
        grid=(b, rows // rows_per_step),
        in_specs=[tile, full, full, tile,
                  pl.BlockSpec(bias.shape, lambda i, j: (0, 0, 0, 0))],
        out_specs=tile,
        out_shape=jax.ShapeDtypeStruct((b, s, NA_WIDTH), BF16),
        compiler_params=_params("arbitrary", "arbitrary"),
        name="na_attn",
    )(q3, k4, v4, gate3, bias)


def _na_bias(rpb):
    near = NA_CBLKS - 1
    reach = near * NA_QBLK
    c = np.arange(GRID_W)
    cs = np.clip(c - NA_KW // 2, 0, GRID_W - NA_KW)
    kc = np.arange(reach)[None, :] + np.where(c < GRID_W // 2, 0, NA_QBLK)[:, None]
    valid = (kc >= cs[:, None]) & (kc < cs[:, None] + NA_KW)
    dc = np.clip(kc - c[:, None] + (NA_KW - 1), 0, 2 * NA_KW - 2)
    n_dr, n_dc = 2 * NA_KH - 1, 2 * NA_KW - 1
    onehot = (dc.reshape(-1)[None, :] == np.arange(n_dc)[:, None]).astype(np.float32)
    sel = jnp.dot(rpb.astype(F32).reshape(NA_HEADS * n_dr, n_dc), jnp.asarray(onehot),
                  precision=lax.Precision.HIGHEST)
    sel = sel.reshape(NA_HEADS, n_dr, GRID_W, reach)
    sel = jnp.where(jnp.asarray(valid)[None, None], sel * LOG2E, NEG)
    per_delta = jnp.stack([sel[:, NA_KH - 1 - d:2 * NA_KH - 1 - d] for d in range(NA_KH)], axis=1)
    b = per_delta.reshape(NA_HEADS, NA_KH, NA_KH, GRID_W, near, NA_QBLK)
    b = b.transpose(0, 1, 3, 4, 2, 5)
    width = near * NA_KH * NA_QBLK
    b = b.reshape(NA_HEADS // 2, 2, NA_KH, GRID_W, width)
    return b.transpose(0, 2, 1, 3, 4).reshape(NA_HEADS // 2, NA_KH, 2 * GRID_W, width)


def _merge_kernel(x_ref, af_ref, ana_ref, km_ref, vm_ref, g_ref, wc_ref, wf_ref, wna_ref, wca_ref,
                  wout_ref, gfin_ref, y_ref, *, subtiles):
    tm = x_ref.shape[0] // subtiles
    m0 = 2 * CA_WIDTH
    for st in range(subtiles):
        rows = slice(st * tm, (st + 1) * tm)
        x = x_ref[rows, :]
        h = _rmsnorm(x, g_ref[...]).astype(BF16)
        gate = lambda j: jax.nn.sigmoid(_dot(h, wc_ref[:, m0 + j * D_MODEL:m0 + (j + 1) * D_MODEL]))
        merged = gate(0) * _dot(af_ref[rows, :], wf_ref[...])
        merged = merged + gate(1) * _dot(ana_ref[rows, :], wna_ref[...])
        z_ca = _dot(h, wc_ref[:, :m0])
        q_ca = (z_ca[:, :CA_WIDTH] * (HEAD_DIM ** -0.5)).astype(BF16)
        parts = []
        for hp in range(CA_HEADS // 2):
            cs = slice(hp * LANES, (hp + 1) * LANES)
            o = _pair_attention(q_ca[:, cs], km_ref[0, :, cs], vm_ref[0, :, cs], None)
            parts.append((o * jax.nn.silu(z_ca[:, CA_WIDTH + hp * LANES:CA_WIDTH + (hp + 1) * LANES])).astype(BF16))
        merged = merged + gate(2) * _dot(jnp.concatenate(parts, axis=1), wca_ref[...])
        out = x + _dot(merged.astype(BF16), wout_ref[...])
        y_ref[rows, :] = _rmsnorm(out, gfin_ref[...])


def _merge(x2, a_f, a_na, kv, g_norm, w_c, w_f, w_na, w_ca, w_out, g_final, seq, tm, subtiles):
    n = x2.shape[0]
    per_batch = seq // tm
    const = lambda i: (0, 0)
    once = dict(pipeline_mode=pl.Buffered(1))
    return pl.pallas_call(
        functools.partial(_merge_kernel, subtiles=subtiles),
        grid=(n // tm,),
        in_specs=[pl.BlockSpec((tm, D_MODEL), lambda i: (i, 0)),
                  pl.BlockSpec((tm, F_WIDTH), lambda i: (i, 0)),
                  pl.BlockSpec((tm, NA_WIDTH), lambda i: (i, 0)),
                  pl.BlockSpec((1, N_MEM, CA_WIDTH), lambda i: (i // per_batch, 0, 0)),
                  pl.BlockSpec((1, N_MEM, CA_WIDTH), lambda i: (i // per_batch, 0, 1)),
                  pl.BlockSpec((1, D_MODEL), const),
                  pl.BlockSpec((D_MODEL, C_WIDTH), const, **once),
                  pl.BlockSpec((F_WIDTH, D_MODEL), const, **once),
                  pl.BlockSpec((NA_WIDTH, D_MODEL), const, **once),
                  pl.BlockSpec((CA_WIDTH, D_MODEL), const, **once),
                  pl.BlockSpec((D_MODEL, D_MODEL), const, **once),
                  pl.BlockSpec((1, D_MODEL), const)],
        out_specs=pl.BlockSpec((tm, D_MODEL), lambda i: (i, 0)),
        out_shape=jax.ShapeDtypeStruct((n, D_MODEL), F32),
        compiler_params=_params("arbitrary"),
        name="merge_out",
    )(x2, a_f, a_na, kv, kv, g_norm, w_c, w_f, w_na, w_ca, w_out, g_final)


def _cos_sin(n):
    k = np.arange(n)
    ang = 2.0 * np.pi * ((k[:, None] * k[None, :]) % n) / n
    return np.cos(ang).astype(np.float32), np.sin(ang).astype(np.float32)


def _row_tables(rows):
    seq = rows * GRID_W
    cr, sr = (jnp.asarray(t)[None] for t in _cos_sin(rows))
    k1, col = np.arange(rows)[None, :, None], np.arange(GRID_W)[:, None, None]
    tw = 2.0 * np.pi * ((k1 * col) % seq) / seq
    ct, st = jnp.asarray(np.cos(tw), F32), jnp.asarray(np.sin(tw), F32)
    c, s = cr * ct - sr * st, sr * ct + cr * st
    top = jnp.concatenate([c, -s], axis=2)
    bottom = jnp.concatenate([-s, -c], axis=2)
    return jnp.concatenate([top, bottom], axis=1).astype(BF16)


def _shared_tables():
    c, s = _cos_sin(F_GROUP_DIM)
    eye = np.eye(F_GROUPS, dtype=np.float32)
    cdft = jnp.asarray(np.concatenate([np.kron(eye, c), np.kron(eye, s)], axis=1)).astype(BF16)
    c, s = _cos_sin(GRID_W)
    eye = np.eye(BF16_SUBLANES, dtype=np.float32)
    return cdft, jnp.asarray(np.kron(c, eye)).astype(BF16), jnp.asarray(np.kron(s, eye)).astype(BF16)


def _trunk(x, mem, g_norm, w_a, w_c, bias, g_mem, w_kv, w_f, w_na, w_ca, w_out, g_final, tables):
    cdft, kc, ks = tables
    b, seq, d = x.shape
    rows = seq // GRID_W
    n = b * seq
    assert d == D_MODEL and seq % GRID_W == 0 and rows >= NA_KH
    assert seq % IN_PROJ_TOKENS == 0 and seq % MERGE_TOKENS == 0 and rows % NA_ROWS_PER_STEP == 0
    assert NA_ROWS_PER_STEP % NA_ROWS_PER_GROUP == 0 and MERGE_TOKENS % MERGE_SUBTILES == 0
    x2 = x.reshape(n, d)

    kv = _mem_kv(mem, g_mem, w_kv)
    p, q, gate_f, qn, kn, vn, gate_na = _in_proj(x2, g_norm, w_a, cdft, seq, tm=IN_PROJ_TOKENS)

    a_f = _fft(p, q, gate_f.reshape(b, GRID_W, rows, F_WIDTH), _row_tables(rows), kc, ks,
               norm=float((seq * F_GROUP_DIM) ** -0.5))

    shp = (b, seq, NA_WIDTH)
    a_na = _na(qn.reshape(shp), kn, vn, gate_na.reshape(shp), bias, rows_per_step=NA_ROWS_PER_STEP)

    y = _merge(x2, a_f.reshape(n, F_WIDTH), a_na.reshape(n, NA_WIDTH), kv, g_norm, w_c, w_f, w_na, w_ca,
               w_out, g_final, seq, tm=MERGE_TOKENS, subtiles=MERGE_SUBTILES)
    return y.reshape(b, seq, d)


def kernel(x_prompt, x_sample, mem_prompt, mem_sample, g_norm, w_in, na_rpb, g_mem, w_mem_kv,
           w_f_out, w_na_out, w_ca_out, w_out, g_final):
    assert w_in.shape[0] == 1, "the fused final norm assumes a single layer"
    w_a = w_in[0, :, :A_WIDTH].astype(BF16)
    w_c = w_in[0, :, A_WIDTH:].astype(BF16)
    args = (g_norm[0][None], w_a, w_c, _na_bias(na_rpb[0]), g_mem[0][None], w_mem_kv[0].astype(BF16),
            w_f_out[0].astype(BF16), w_na_out[0].astype(BF16), w_ca_out[0].astype(BF16),
            w_out[0].astype(BF16), g_final[None], _shared_tables())
    return (_trunk(x_prompt, mem_prompt, *args), _trunk(x_sample, mem_sample, *args))
```
